```python
import jax, jax.numpy as jnp
from jax import lax
import numpy as np

D_MODEL = 2048
BATCH = 8
SEQ = 2048
DEPTH = 2
DEC_BATCH = 32
DEC_SEQ = 8
PAST_LEN = 8192
PAGE_SIZE = 128

N_META = 16
QBLOCK = 128
HEAD_DIM = 128
N_BRANCH = 4
MIX_W = D_MODEL // 4
A_HEADS = MIX_W // HEAD_DIM
B_HEADS = MIX_W // HEAD_DIM
C_HEADS = MIX_W // HEAD_DIM
C_KV_DIM = HEAD_DIM
IDX_HEADS = 8
IDX_DIM = 64
TOPK_MAX = 256
D_WIDTH = MIX_W
RG_BLOCKS = 8
RG_BW = D_WIDTH // RG_BLOCKS
RG_C = 8.0
CONV_W = 4
D_FF = ((8 * D_MODEL + 3 * 256 - 1) // (3 * 256)) * 256
RMS_EPS = 1e-6
N_IN = (3 * A_HEADS * HEAD_DIM + A_HEADS + 3 * B_HEADS * HEAD_DIM + C_HEADS * HEAD_DIM
        + 2 * C_KV_DIM + IDX_HEADS * IDX_DIM + IDX_DIM + IDX_HEADS + D_WIDTH)

kernel_name = 'hybrid_fox_sb_dsa_rglru_step'


def proj_widths():
    return ((A_HEADS * HEAD_DIM,) * 3 + (A_HEADS,) + (B_HEADS * HEAD_DIM,) * 3
            + (C_HEADS * HEAD_DIM, C_KV_DIM, C_KV_DIM, IDX_HEADS * IDX_DIM, IDX_DIM, IDX_HEADS, D_WIDTH))


def dsa_topk(n_visible):
    return min(TOPK_MAX, n_visible // 4)


def rmsnorm(x, g):
    xf = x.astype(jnp.float32)
    y = xf * lax.rsqrt(jnp.mean(xf * xf, axis=-1, keepdims=True) + RMS_EPS)
    return y.astype(x.dtype) * g


def sweep_queries(fn, qs, q_pos):
    t = q_pos.shape[0]
    lead = t % QBLOCK
    outs = []
    if lead:
        outs.append(fn(tuple(a[:, :lead] for a in qs), q_pos[:lead]))
    nb = (t - lead) // QBLOCK
    if nb:
        blk = tuple(jnp.moveaxis(a[:, lead:].reshape((a.shape[0], nb, QBLOCK) + a.shape[2:]), 1, 0) for a in qs)
        pb = q_pos[lead:].reshape(nb, QBLOCK)
        o = lax.map(lambda args: fn(args[0], args[1]), (blk, pb))
        o = jnp.moveaxis(o, 0, 1)
        outs.append(o.reshape((o.shape[0], nb * QBLOCK) + o.shape[3:]))
    return jnp.concatenate(outs, axis=1) if len(outs) > 1 else outs[0]


def fox_attention(q, k, v, f_cum, q_pos):
    n_keys, t_new = k.shape[1], q.shape[1]
    k_pos = jnp.arange(n_keys)
    fk = jnp.moveaxis(f_cum, 1, 2)
    scale = HEAD_DIM ** -0.5

    def block(qs, pos):
        qb, fqb = qs
        s = jnp.einsum('bqhd,bshd->bhqs', qb, k, preferred_element_type=jnp.float32) * scale
        s = s + jnp.moveaxis(fqb, 1, 2)[..., None] - fk[:, :, None, :]
        s = jnp.where(k_pos[None, :] <= pos[:, None], s, -jnp.inf)
        p = jax.nn.softmax(s, axis=-1)
        return jnp.einsum('bhqs,bshd->bqhd', p, v).astype(q.dtype)

    return sweep_queries(block, (q, f_cum[:, n_keys - t_new:]), q_pos)


def stick_breaking_attention(q, k, v, q_pos):
    k_pos = jnp.arange(k.shape[1])
    scale = HEAD_DIM ** -0.5

    def block(qs, pos):
        (qb,) = qs
        z = jnp.einsum('bqhd,bshd->bhqs', qb, k, preferred_element_type=jnp.float32) * scale
        vis = k_pos[None, :] < pos[:, None]
        log_keep = jnp.where(vis, jax.nn.log_sigmoid(-z), 0.0)
        later = lax.cumsum(log_keep, axis=3, reverse=True) - log_keep
        att = jnp.where(vis, jnp.exp(jax.nn.log_sigmoid(z) + later), 0.0)
        return jnp.einsum('bhqs,bshd->bqhd', att, v).astype(q.dtype)

    return sweep_queries(block, (q,), q_pos)


def dsa_attention(q, k, v, qi, ki, wi, q_pos, topk):
    k_pos = jnp.arange(k.shape[1])
    scale = HEAD_DIM ** -0.5
    idx_scale = (IDX_DIM * IDX_HEADS) ** -0.5
    gather = jax.vmap(lambda rows, idx: rows[idx])

    def block(qs, pos):
        qb, qib, wib = qs
        rel = jax.nn.relu(jnp.einsum('bqhd,bsd->bqhs', qib, ki, preferred_element_type=jnp.float32))
        score = jnp.einsum('bqh,bqhs->bqs', wib.astype(jnp.float32), rel) * idx_scale
        score = jnp.where((k_pos[None, :] <= pos[:, None])[None], score, -jnp.inf)
        _, idx = lax.top_k(score, topk)
        sel_ok = idx <= pos[None, :, None]
        kg = gather(k, idx)
        vg = gather(v, idx)
        s = jnp.einsum('bqhd,bqkd->bqhk', qb, kg, preferred_element_type=jnp.float32) * scale
        s = jnp.where(sel_ok[:, :, None, :], s, -jnp.inf)
        p = jax.nn.softmax(s, axis=-1)
        return jnp.einsum('bqhk,bqkd->bqhd', p, vg).astype(q.dtype)

    return sweep_queries(block, (q, qi, wi), q_pos)


def rg_lru(xd, conv_state, h0, conv_w, conv_b, w_a, b_a, w_x, b_x, lam):
    n_b, t_new, width = xd.shape
    xpad = jnp.concatenate([conv_state.astype(xd.dtype), xd], axis=1)
    xc = conv_b + sum(xpad[:, i:i + t_new] * conv_w[i] for i in range(CONV_W))
    xb = xc.reshape(n_b, t_new, RG_BLOCKS, RG_BW)
    gate_r = jax.nn.sigmoid((jnp.einsum('btnc,nce->btne', xb, w_a).reshape(n_b, t_new, width) + b_a).astype(jnp.float32))
    gate_i = jax.nn.sigmoid((jnp.einsum('btnc,nce->btne', xb, w_x).reshape(n_b, t_new, width) + b_x).astype(jnp.float32))
    log_a = -RG_C * gate_r * jax.nn.softplus(-lam.astype(jnp.float32))
    a = jnp.exp(log_a)
    u = jnp.sqrt(-jnp.expm1(2.0 * log_a)) * gate_i * xc.astype(jnp.float32)

    def step(h, au):
        h = au[0] * h + au[1]
        return h, h

    h_last, hs = lax.scan(step, h0.astype(jnp.float32), (jnp.moveaxis(a, 1, 0), jnp.moveaxis(u, 1, 0)))
    return jnp.moveaxis(hs, 0, 1).astype(xd.dtype), xpad[:, t_new:], h_last.astype(xd.dtype)


def trunk_layer(x, past, w, topk):
    (p_ak, p_av, p_alf, p_bk, p_bv, p_ck, p_cv, p_cik, conv_state, h0) = past
    (w_in, b_f, conv_w, conv_b, w_a, b_a, w_x, b_x, lam,
     w_gate, w_branch, w_out, g_mix, g_ffn, w_fg, w_fu, w_fd) = w
    n_b, t_new, _ = x.shape
    q_pos = p_ak.shape[1] + jnp.arange(t_new)
    h = rmsnorm(x, g_mix)
    (qa, ka, va, fa, qb, kb, vb, qc, kc, vc, qi, ki, wi, xd) = jnp.split(
        h @ w_in, np.cumsum(proj_widths())[:-1].tolist(), axis=-1)

    def heads(t, n):
        return t.reshape(n_b, t_new, n, -1)

    ka, va = heads(ka, A_HEADS), heads(va, A_HEADS)
    logf = jax.nn.log_sigmoid((fa + b_f).astype(jnp.float32))
    f_cum = jnp.cumsum(jnp.concatenate([p_alf.astype(jnp.float32), logf], axis=1), axis=1)
    y_a = fox_attention(heads(qa, A_HEADS), jnp.concatenate([p_ak, ka], 1),
                        jnp.concatenate([p_av, va], 1), f_cum, q_pos)
    kb, vb = heads(kb, B_HEADS), heads(vb, B_HEADS)
    y_b = stick_breaking_attention(heads(qb, B_HEADS), jnp.concatenate([p_bk, kb], 1),
                                   jnp.concatenate([p_bv, vb], 1), q_pos)
    y_c = dsa_attention(heads(qc, C_HEADS), jnp.concatenate([p_ck, kc], 1), jnp.concatenate([p_cv, vc], 1),
                        heads(qi, IDX_HEADS), jnp.concatenate([p_cik, ki], 1), wi, q_pos, topk)
    y_d, conv_new, h_new = rg_lru(xd, conv_state, h0, conv_w, conv_b, w_a, b_a, w_x, b_x, lam)

    ys = jnp.stack([y_a.reshape(n_b, t_new, MIX_W), y_b.reshape(n_b, t_new, MIX_W),
                    y_c.reshape(n_b, t_new, MIX_W), y_d], axis=2)
    gates = jax.nn.sigmoid(h @ w_gate).reshape(n_b, t_new, N_BRANCH, D_MODEL)
    merged = jnp.sum(gates * jnp.einsum('btnw,nwd->btnd', ys, w_branch), axis=2)
    x = x + merged @ w_out
    h2 = rmsnorm(x, g_ffn)
    x = x + (jax.nn.silu(h2 @ w_fg) * (h2 @ w_fu)) @ w_fd
    new_rows = (ka, va, logf.astype(x.dtype), kb, vb, kc, vc, ki, conv_new, h_new)
    return x, new_rows


def setup_inputs(seed: int = 0) -> dict:
    key = jax.random.key(seed)
    ks = iter(jax.random.split(key, 40))
    f32 = jnp.float32
    n_pages = PAST_LEN // PAGE_SIZE
    n_pool = (5 * DEC_BATCH * n_pages + 3) // 4

    def nrm(shape, scale=1.0):
        return scale * jax.random.normal(next(ks), shape, f32)

    x_prompt = nrm((BATCH, SEQ, D_MODEL))
    x_sample = nrm((DEC_BATCH, DEC_SEQ, D_MODEL))
    cache_a_k = nrm((DEPTH, n_pool, PAGE_SIZE, A_HEADS, HEAD_DIM))
    cache_a_v = nrm((DEPTH, n_pool, PAGE_SIZE, A_HEADS, HEAD_DIM))
    cache_a_logf = jax.nn.log_sigmoid(2.0 + nrm((DEPTH, n_pool, PAGE_SIZE, A_HEADS)))
    cache_b_k = nrm((DEPTH, n_pool, PAGE_SIZE, B_HEADS, HEAD_DIM))
    cache_b_v = nrm((DEPTH, n_pool, PAGE_SIZE, B_HEADS, HEAD_DIM))
    cache_c_k = nrm((DEPTH, n_pool, PAGE_SIZE, C_KV_DIM))
    cache_c_v = nrm((DEPTH, n_pool, PAGE_SIZE, C_KV_DIM))
    cache_c_idx_k = nrm((DEPTH, n_pool, PAGE_SIZE, IDX_DIM))
    state_d_conv = nrm((DEPTH, DEC_BATCH, CONV_W - 1, D_WIDTH))
    state_d_h = nrm((DEPTH, DEC_BATCH, D_WIDTH), 0.5)
    page_table = jax.random.permutation(next(ks), n_pool)[:DEC_BATCH * n_pages].reshape(
        DEC_BATCH, n_pages).astype(jnp.int32)
    meta_tokens = nrm((N_META, D_MODEL))
    w_in = nrm((DEPTH, D_MODEL, N_IN), D_MODEL ** -0.5)
    b_forget = 2.0 + nrm((DEPTH, A_HEADS), 0.5)
    conv_w = nrm((DEPTH, CONV_W, D_WIDTH), CONV_W ** -0.5)
    conv_b = nrm((DEPTH, D_WIDTH), 0.01)
    w_rg_a = nrm((DEPTH, RG_BLOCKS, RG_BW, RG_BW), RG_BW ** -0.5)
    b_rg_a = nrm((DEPTH, D_WIDTH), 0.01)
    w_rg_x = nrm((DEPTH, RG_BLOCKS, RG_BW, RG_BW), RG_BW ** -0.5)
    b_rg_x = nrm((DEPTH, D_WIDTH), 0.01)
    a_pow = jax.random.uniform(next(ks), (DEPTH, D_WIDTH), f32, minval=0.9, maxval=0.999)
    a_base = a_pow ** (1.0 / RG_C)
    rg_lambda = jnp.log(a_base) - jnp.log1p(-a_base)
    w_gate = nrm((DEPTH, D_MODEL, N_BRANCH * D_MODEL), D_MODEL ** -0.5)
    w_branch = nrm((DEPTH, N_BRANCH, MIX_W, D_MODEL), MIX_W ** -0.5)
    w_out = nrm((DEPTH, D_MODEL, D_MODEL), D_MODEL ** -0.5)
    norm_mix = 1.0 + nrm((DEPTH, D_MODEL), 0.01)
    norm_ffn = 1.0 + nrm((DEPTH, D_MODEL), 0.01)
    w_ffn_gate = nrm((DEPTH, D_MODEL, D_FF), D_MODEL ** -0.5)
    w_ffn_up = nrm((DEPTH, D_MODEL, D_FF), D_MODEL ** -0.5)
    w_ffn_down = nrm((DEPTH, D_FF, D_MODEL), D_FF ** -0.5)
    norm_final = 1.0 + nrm((D_MODEL,), 0.01)
    return {'x_prompt': x_prompt, 'x_sample': x_sample,
            'cache_a_k': cache_a_k, 'cache_a_v': cache_a_v, 'cache_a_logf': cache_a_logf,
            'cache_b_k': cache_b_k, 'cache_b_v': cache_b_v,
            'cache_c_k': cache_c_k, 'cache_c_v': cache_c_v, 'cache_c_idx_k': cache_c_idx_k,
            'state_d_conv': state_d_conv, 'state_d_h': state_d_h, 'page_table': page_table,
            'meta_tokens': meta_tokens, 'w_in': w_in, 'b_forget': b_forget,
            'conv_w': conv_w, 'conv_b': conv_b, 'w_rg_a': w_rg_a, 'b_rg_a': b_rg_a,
            'w_rg_x': w_rg_x, 'b_rg_x': b_rg_x, 'rg_lambda': rg_lambda,
            'w_gate': w_gate, 'w_branch': w_branch, 'w_out': w_out,
            'norm_mix': norm_mix, 'norm_ffn': norm_ffn,
            'w_ffn_gate': w_ffn_gate, 'w_ffn_up': w_ffn_up, 'w_ffn_down': w_ffn_down,
            'norm_final': norm_final}


def reference(x_prompt, x_sample, cache_a_k, cache_a_v, cache_a_logf, cache_b_k, cache_b_v,
              cache_c_k, cache_c_v, cache_c_idx_k, state_d_conv, state_d_h, page_table,
              meta_tokens, w_in, b_forget, conv_w, conv_b, w_rg_a, b_rg_a, w_rg_x, b_rg_x, rg_lambda,
              w_gate, w_branch, w_out, norm_mix, norm_ffn, w_ffn_gate, w_ffn_up, w_ffn_down, norm_final):
    def layer_weights(l):
        return (w_in[l], b_forget[l], conv_w[l], conv_b[l], w_rg_a[l], b_rg_a[l], w_rg_x[l], b_rg_x[l],
                rg_lambda[l], w_gate[l], w_branch[l], w_out[l], norm_mix[l], norm_ffn[l],
                w_ffn_gate[l], w_ffn_up[l], w_ffn_down[l])

    n_p, seq_p, _ = x_prompt.shape
    dt = x_prompt.dtype
    xp = jnp.concatenate([jnp.broadcast_to(meta_tokens.astype(dt)[None], (n_p, N_META, D_MODEL)), x_prompt], axis=1)
    prompt_past = (jnp.zeros((n_p, 0, A_HEADS, HEAD_DIM), dt), jnp.zeros((n_p, 0, A_HEADS, HEAD_DIM), dt),
                   jnp.zeros((n_p, 0, A_HEADS), dt),
                   jnp.zeros((n_p, 0, B_HEADS, HEAD_DIM), dt), jnp.zeros((n_p, 0, B_HEADS, HEAD_DIM), dt),
                   jnp.zeros((n_p, 0, C_KV_DIM), dt), jnp.zeros((n_p, 0, C_KV_DIM), dt),
                   jnp.zeros((n_p, 0, IDX_DIM), dt),
                   jnp.zeros((n_p, CONV_W - 1, D_WIDTH), dt), jnp.zeros((n_p, D_WIDTH), dt))
    topk_p = dsa_topk(seq_p)
    prompt_rows = []
    for l in range(DEPTH):
        xp, rows = trunk_layer(xp, prompt_past, layer_weights(l), topk_p)
        prompt_rows.append(rows)
    y_prompt = rmsnorm(xp, norm_final)[:, N_META:]

    n_dec, t_dec, _ = x_sample.shape

    def paged(cache, l):
        rows = cache[l][page_table]
        return rows.reshape((n_dec, -1) + rows.shape[3:])

    past_len = page_table.shape[1] * cache_a_k.shape[2]
    topk_s = dsa_topk(past_len + t_dec)
    xs = x_sample
    sample_rows = []
    for l in range(DEPTH):
        past = (paged(cache_a_k, l), paged(cache_a_v, l), paged(cache_a_logf, l),
                paged(cache_b_k, l), paged(cache_b_v, l),
                paged(cache_c_k, l), paged(cache_c_v, l), paged(cache_c_idx_k, l),
                state_d_conv[l], state_d_h[l])
        xs, rows = trunk_layer(xs, past, layer_weights(l), topk_s)
        sample_rows.append(rows)
    y_sample = rmsnorm(xs, norm_final)

    (p_a_k, p_a_v, p_a_logf, p_b_k, p_b_v, p_c_k, p_c_v, p_c_idx_k, p_d_conv, p_d_h) = [
        jnp.stack(list(t)) for t in zip(*prompt_rows)]
    (s_a_k, s_a_v, s_a_logf, s_b_k, s_b_v, s_c_k, s_c_v, s_c_idx_k, s_d_conv, s_d_h) = [
        jnp.stack(list(t)) for t in zip(*sample_rows)]
    return (y_prompt, y_sample,
            p_a_k, p_a_v, p_a_logf, p_b_k, p_b_v, p_c_k, p_c_v, p_c_idx_k, p_d_conv, p_d_h,
            s_a_k, s_a_v, s_a_logf, s_b_k, s_b_v, s_c_k, s_c_v, s_c_idx_k, s_d_conv, s_d_h)
```

```python
import functools

import jax
import jax.numpy as jnp
from jax import lax
from jax.experimental import pallas as pl
from jax.experimental.pallas import tpu as pltpu

F32 = jnp.float32
BF16 = jnp.bfloat16
I32 = jnp.int32

D_MODEL = 2048
N_META = 16
HEAD_DIM = 128
N_HEADS = 4
MIX_W = 512
IDX_HEADS = 8
IDX_DIM = 64
TOPK_MAX = 256
RG_BLOCKS = 8
RG_C = 8.0
CONV_W = 4
RMS_EPS = 1e-6
PAGE = 128

LANES = 128
SUBLANES = 8
CHUNK = 128
VMEM_LIMIT = 56 * 1024 * 1024

OFF_QA, OFF_KA, OFF_VA = 0, 512, 1024
OFF_QB, OFF_KB, OFF_VB = 1536, 2048, 2560
OFF_QC, OFF_QI, OFF_XD = 3072, 3584, 4096
OFF_KC, OFF_VC, OFF_MISC = 4608, 4736, 4864
MISC_KI, MISC_FA, MISC_WI = 0, 64, 68
N_PROJ = 5120

NEG_INF = float("-inf")
INT_MIN = -2 ** 31


def _cparams(*sem):
    return pltpu.CompilerParams(dimension_semantics=sem, vmem_limit_bytes=VMEM_LIMIT)


def _split3(x):
    x1 = x.astype(BF16)
    r1 = x - x1.astype(F32)
    x2 = r1.astype(BF16)
    x3 = (r1 - x2.astype(F32)).astype(BF16)
    return x1, x2, x3


def _dot01_left(m01, x):
    return sum(jnp.dot(m01, p, preferred_element_type=F32) for p in _split3(x))


def _dot01_right(x, m01):
    return sum(jnp.dot(p, m01, preferred_element_type=F32) for p in _split3(x))


def _dot_nt(a, b):
    return lax.dot_general(a, b, (((1,), (1,)), ((), ())), preferred_element_type=F32)


def _softplus(x):
    return jnp.maximum(x, 0.0) + jnp.log1p(jnp.exp(-jnp.abs(x)))


def _iota(shape, dim):
    return lax.broadcasted_iota(I32, shape, dim)


def _norm_proj_kernel(x_ref, g_ref, w_ref, o32_ref, o16_ref, h16_ref, h_scr):
    @pl.when(pl.program_id(1) == 0)
    def _():
        x = x_ref[...]
        y = x * lax.rsqrt(jnp.mean(x * x, axis=-1, keepdims=True) + RMS_EPS)
        h = (y * g_ref[...]).astype(BF16)
        h_scr[...] = h
        h16_ref[...] = h

    acc = jnp.dot(h_scr[...], w_ref[...], preferred_element_type=F32)
    o32_ref[...] = acc
    o16_ref[...] = acc.astype(BF16)


def _row_tile(m, pref):
    for t in range(min(pref, m), 7, -1):
        if m % t == 0 and t % 8 == 0:
            return t
    raise ValueError(f"no row tile for {m}")


def norm_proj(x, g, w16, tm_pref=512, tn=512):
    m, d = x.shape
    n = w16.shape[1]
    tm = _row_tile(m, tm_pref)
    return pl.pallas_call(
        _norm_proj_kernel,
        grid=(m // tm, n // tn),
        in_specs=[pl.BlockSpec((tm, d), lambda i, j: (i, 0)),
                  pl.BlockSpec((1, d), lambda i, j: (0, 0)),
                  pl.BlockSpec((d, tn), lambda i, j: (0, j))],
        out_specs=[pl.BlockSpec((tm, tn), lambda i, j: (i, j)),
                   pl.BlockSpec((tm, tn), lambda i, j: (i, j)),
                   pl.BlockSpec((tm, d), lambda i, j: (i, 0))],
        out_shape=[jax.ShapeDtypeStruct((m, n), F32),
                   jax.ShapeDtypeStruct((m, n), BF16),
                   jax.ShapeDtypeStruct((m, d), BF16)],
        scratch_shapes=[pltpu.VMEM((tm, d), BF16)],
        compiler_params=_cparams("parallel", "arbitrary"),
        name="norm_proj",
    )(x, g.reshape(1, d), w16)


def _logf_kernel(misc_ref, bias_ref, carry_ref, logf_ref, fcum_ref, pad_scr, *, t, n_chunks):
    lane = _iota((1, LANES), 1)
    live = (lane >= MISC_FA) & (lane < MISC_FA + N_HEADS)
    logf = jnp.where(live, jax.nn.log_sigmoid(misc_ref[0] + bias_ref[...]), 0.0)
    logf_ref[0] = logf
    pad_scr[pl.ds(0, t), :] = logf
    if n_chunks * CHUNK > t:
        pad_scr[pl.ds(t, n_chunks * CHUNK - t), :] = jnp.zeros((n_chunks * CHUNK - t, LANES), F32)
    tril = (_iota((CHUNK, CHUNK), 1) <= _iota((CHUNK, CHUNK), 0)).astype(BF16)
    carry = carry_ref[0]
    for c in range(n_chunks):
        f = _dot01_left(tril, pad_scr[pl.ds(c * CHUNK, CHUNK), :]) + carry
        rows = min(CHUNK, t - c * CHUNK)
        fcum_ref[0, pl.ds(c * CHUNK, rows), :] = f[:rows]
        carry = f[CHUNK - 1:CHUNK, :]


def logf_cumsum(misc, bias_row, carry):
    b, t, _ = misc.shape
    n_chunks = pl.cdiv(t, CHUNK)
    return pl.pallas_call(
        functools.partial(_logf_kernel, t=t, n_chunks=n_chunks),
        grid=(b,),
        in_specs=[pl.BlockSpec((1, t, LANES), lambda i: (i, 0, 0)),
                  pl.BlockSpec((1, LANES), lambda i: (0, 0)),
                  pl.BlockSpec((1, 1, LANES), lambda i: (i, 0, 0))],
        out_specs=[pl.BlockSpec((1, t, LANES), lambda i: (i, 0, 0)),
                   pl.BlockSpec((1, t, LANES), lambda i: (i, 0, 0))],
        out_shape=[jax.ShapeDtypeStruct((b, t, LANES), F32)] * 2,
        scratch_shapes=[pltpu.VMEM((n_chunks * CHUNK, LANES), F32)],
        compiler_params=_cparams("parallel"),
        name="logf_cumsum",
    )(misc, bias_row, carry)


def _pad_copy(dst, src, t):
    rows = dst.shape[0]
    dst[pl.ds(0, t), :] = src
    if rows > t:
        dst[pl.ds(t, rows - t), :] = jnp.zeros((rows - t, dst.shape[1]), dst.dtype)


def _query_tiles(t):
    n_full = t // CHUNK
    return n_full, t - n_full * CHUNK


def _fox_prompt_kernel(q_ref, k_ref, v_ref, fcol_ref, frow_ref, o_ref, k_scr, v_scr, *, t):
    n_full, tail = _query_tiles(t)
    scale = HEAD_DIM ** -0.5
    _pad_copy(k_scr, k_ref[0], t)
    _pad_copy(v_scr, v_ref[0], t)

    def tile(h, row0, tq, diag):
        hs = slice(h * HEAD_DIM, (h + 1) * HEAD_DIM)
        q = q_ref[0, pl.ds(row0, tq), hs]
        fq = fcol_ref[0, pl.ds(row0, tq), h:h + 1]

        def chunk(j, carry, masked):
            m, l, acc = carry
            k0 = pl.multiple_of(j * CHUNK, CHUNK)
            s = _dot_nt(q, k_scr[pl.ds(k0, CHUNK), hs]) * scale
            s = s + fq - frow_ref[0, j, h:h + 1, :]
            if masked:
                s = jnp.where(_iota((tq, CHUNK), 1) <= _iota((tq, CHUNK), 0), s, NEG_INF)
            m_new = jnp.maximum(m, jnp.max(s, axis=1, keepdims=True))
            alpha = jnp.exp(m - m_new)
            p = jnp.exp(s - m_new)
            l = alpha * l + jnp.sum(p, axis=1, keepdims=True)
            acc = alpha * acc + jnp.dot(p.astype(BF16), v_scr[pl.ds(k0, CHUNK), hs],
                                        preferred_element_type=F32)
            return m_new, l, acc

        init = (jnp.full((tq, 1), NEG_INF, F32), jnp.zeros((tq, 1), F32), jnp.zeros((tq, HEAD_DIM), F32))
        carry = chunk(diag, init, True)
        carry = lax.fori_loop(0, diag, lambda j, c: chunk(j, c, False), carry)
        _, l, acc = carry
        o_ref[0, pl.ds(row0, tq), hs] = (acc / l).astype(o_ref.dtype)

    for h in range(N_HEADS):
        def body(i, _):
            tile(h, pl.multiple_of(i * CHUNK, CHUNK), CHUNK, i)
            return 0
        lax.fori_loop(0, n_full, body, 0)
        if tail:
            tile(h, n_full * CHUNK, tail, n_full)


def _frow_chunks(fcol):
    b, t, h = fcol.shape
    n_chunks = pl.cdiv(t, CHUNK)
    f = jnp.pad(fcol, ((0, 0), (0, n_chunks * CHUNK - t), (0, 0)))
    return f.reshape(b, n_chunks, CHUNK, h).transpose(0, 1, 3, 2)


def fox_prompt(proj16, fcol, frow_chunks):
    b, t, _ = proj16.shape
    n_chunks = pl.cdiv(t, CHUNK)
    tp = n_chunks * CHUNK
    blk = lambda off: pl.BlockSpec((1, t, MIX_W), lambda i, o=off // MIX_W: (i, 0, o))
    return pl.pallas_call(
        functools.partial(_fox_prompt_kernel, t=t),
        grid=(b,),
        in_specs=[blk(OFF_QA), blk(OFF_KA), blk(OFF_VA),
                  pl.BlockSpec((1, t, N_HEADS), lambda i: (i, 0, 0)),
                  pl.BlockSpec((1, n_chunks, N_HEADS, CHUNK), lambda i: (i, 0, 0, 0))],
        out_specs=pl.BlockSpec((1, t, MIX_W), lambda i: (i, 0, 0)),
        out_shape=jax.ShapeDtypeStruct((b, t, MIX_W), BF16),
        scratch_shapes=[pltpu.VMEM((tp, MIX_W), BF16), pltpu.VMEM((tp, MIX_W), BF16)],
        compiler_params=_cparams("parallel"),
        name="fox_prompt",
    )(proj16, proj16, proj16, fcol, frow_chunks)


def _suffix_matrix():
    return (_iota((CHUNK, CHUNK), 0) > _iota((CHUNK, CHUNK), 1)).astype(BF16)


def _sb_prompt_kernel(q_ref, k_ref, v_ref, o_ref, k_scr, v_scr, *, t):
    n_full, tail = _query_tiles(t)
    scale = HEAD_DIM ** -0.5
    _pad_copy(k_scr, k_ref[0], t)
    _pad_copy(v_scr, v_ref[0], t)
    suffix = _suffix_matrix()

    def tile(h, row0, tq, diag):
        hs = slice(h * HEAD_DIM, (h + 1) * HEAD_DIM)
        q = q_ref[0, pl.ds(row0, tq), hs]

        def chunk(j, carry, masked):
            later_chunks, acc = carry
            k0 = pl.multiple_of(j * CHUNK, CHUNK)
            z = _dot_nt(q, k_scr[pl.ds(k0, CHUNK), hs]) * scale
            log_keep = -_softplus(z)
            log_att = log_keep + z
            if masked:
                vis = _iota((tq, CHUNK), 1) < _iota((tq, CHUNK), 0)
                log_keep = jnp.where(vis, log_keep, 0.0)
            later = _dot01_right(log_keep, suffix) + later_chunks
            att = jnp.exp(log_att + later)
            if masked:
                att = jnp.where(vis, att, 0.0)
            acc = acc + jnp.dot(att.astype(BF16), v_scr[pl.ds(k0, CHUNK), hs], preferred_element_type=F32)
            return later_chunks + jnp.sum(log_keep, axis=1, keepdims=True), acc

        carry = chunk(diag, (jnp.zeros((tq, 1), F32), jnp.zeros((tq, HEAD_DIM), F32)), True)
        carry = lax.fori_loop(0, diag, lambda r, c: chunk(diag - 1 - r, c, False), carry)
        o_ref[0, pl.ds(row0, tq), hs] = carry[1].astype(o_ref.dtype)

    for h in range(N_HEADS):
        def body(i, _):
            tile(h, pl.multiple_of(i * CHUNK, CHUNK), CHUNK, i)
            return 0
        lax.fori_loop(0, n_full, body, 0)
        if tail:
            tile(h, n_full * CHUNK, tail, n_full)


def sb_prompt(proj16):
    b, t, _ = proj16.shape
    tp = pl.cdiv(t, CHUNK) * CHUNK
    blk = lambda off: pl.BlockSpec((1, t, MIX_W), lambda i, o=off // MIX_W: (i, 0, o))
    return pl.pallas_call(
        functools.partial(_sb_prompt_kernel, t=t),
        grid=(b,),
        in_specs=[blk(OFF_QB), blk(OFF_KB), blk(OFF_VB)],
        out_specs=pl.BlockSpec((1, t, MIX_W), lambda i: (i, 0, 0)),
        out_shape=jax.ShapeDtypeStruct((b, t, MIX_W), BF16),
        scratch_shapes=[pltpu.VMEM((tp, MIX_W), BF16), pltpu.VMEM((tp, MIX_W), BF16)],
        compiler_params=_cparams("parallel"),
        name="sb_prompt",
    )(proj16, proj16, proj16)


def _order_key(score):
    score = jnp.where(score == 0.0, 0.0, score)
    bits = lax.bitcast_convert_type(score, I32)
    return bits ^ ((bits >> 31) & 0x7FFFFFFF)


def _select_topk(key_scr, bias_scr, vis_fn, n_chunks, tq, topk):
    kf = float(topk)

    def count(pred):
        def body(j, part):
            return part + jnp.where(pred(key_scr[j, 0:tq, :]), 1.0, 0.0)
        part = lax.fori_loop(0, n_chunks, body, jnp.zeros((tq, CHUNK), F32))
        return jnp.sum(part, axis=1, keepdims=True)

    nonneg = count(lambda k: k >= 0)
    thr0 = jnp.where(nonneg >= kf, 0, INT_MIN).astype(I32)

    def bit_step(b, thr):
        cand = thr | (jnp.int32(1) << (30 - b))
        return jnp.where(count(lambda k: k >= cand) >= kf, cand, thr)

    thr = lax.fori_loop(0, 31, bit_step, thr0)
    n_gt = count(lambda k: k > thr)
    need = kf - n_gt

    def tie_count(j, part):
        return part + jnp.where((key_scr[j, 0:tq, :] == thr) & vis_fn(j), 1.0, 0.0)
    n_tie = jnp.sum(lax.fori_loop(0, n_chunks, tie_count, jnp.zeros((tq, CHUNK), F32)), axis=1, keepdims=True)

    def plain(j, _):
        sel = (key_scr[j, 0:tq, :] >= thr) & vis_fn(j)
        bias_scr[j, 0:tq, :] = jnp.where(sel, 0.0, NEG_INF)
        return 0
    lax.fori_loop(0, n_chunks, plain, 0)

    @pl.when(jnp.max(n_tie - need) > 0.0)
    def _():
        prefix = (_iota((CHUNK, CHUNK), 0) <= _iota((CHUNK, CHUNK), 1)).astype(BF16)

        def ranked(j, seen):
            k = key_scr[j, 0:tq, :]
            tie = (k == thr) & vis_fn(j)
            rank = jnp.dot(jnp.where(tie, 1.0, 0.0).astype(BF16), prefix, preferred_element_type=F32) + seen
            sel = ((k > thr) & vis_fn(j)) | (tie & (rank <= need))
            bias_scr[j, 0:tq, :] = jnp.where(sel, 0.0, NEG_INF)
            return seen + jnp.sum(jnp.where(tie, 1.0, 0.0), axis=1, keepdims=True)
        lax.fori_loop(0, n_chunks, ranked, jnp.zeros((tq, 1), F32))


def _indexer_scores(qi, wi, ki):
    score = None
    for h in range(IDX_HEADS):
        rel = jnp.maximum(_dot_nt(qi[:, h * IDX_DIM:(h + 1) * IDX_DIM], ki), 0.0)
        term = wi[:, h:h + 1] * rel
        score = term if score is None else score + term
    return score * ((IDX_DIM * IDX_HEADS) ** -0.5)


def _stack_heads(q):
    return jnp.concatenate([q[:, h * HEAD_DIM:(h + 1) * HEAD_DIM] for h in range(N_HEADS)], axis=0)


def _masked_softmax_step(carry, s, v):
    m, l, acc = carry
    m_new = jnp.maximum(m, jnp.max(s, axis=1, keepdims=True))
    m_safe = jnp.where(m_new == NEG_INF, 0.0, m_new)
    alpha = jnp.exp(m - m_safe)
    p = jnp.exp(s - m_safe)
    l = alpha * l + jnp.sum(p, axis=1, keepdims=True)
    acc = alpha * acc + jnp.dot(p.astype(BF16), v, preferred_element_type=F32)
    return m_new, l, acc


def _softmax_init(rows, width):
    return (jnp.full((rows, 1), NEG_INF, F32), jnp.zeros((rows, 1), F32), jnp.zeros((rows, width), F32))


def _dsa_prompt_kernel(q_ref, qi_ref, kc_ref, vc_ref, misc32_ref, misc16_ref, o_ref,
                       kc_scr, vc_scr, ki_scr, key_scr, bias_scr, *, t, topk):
    n_full, tail = _query_tiles(t)
    scale = HEAD_DIM ** -0.5
    _pad_copy(kc_scr, kc_ref[0], t)
    _pad_copy(vc_scr, vc_ref[0], t)
    _pad_copy(ki_scr, misc16_ref[0], t)

    def tile(row0, tq, diag):
        n_chunks = diag + 1
        qi = qi_ref[0, pl.ds(row0, tq), :]
        wi = misc32_ref[0, pl.ds(row0, tq), MISC_WI:MISC_WI + IDX_HEADS]

        def vis_fn(j):
            return (j * CHUNK + _iota((tq, CHUNK), 1)) <= (row0 + _iota((tq, CHUNK), 0))

        def score_chunk(j, _):
            k0 = pl.multiple_of(j * CHUNK, CHUNK)
            score = _indexer_scores(qi, wi, ki_scr[pl.ds(k0, CHUNK), MISC_KI:MISC_KI + IDX_DIM])
            key_scr[j, 0:tq, :] = jnp.where(vis_fn(j), _order_key(score), INT_MIN)
            return 0
        lax.fori_loop(0, n_chunks, score_chunk, 0)

        _select_topk(key_scr, bias_scr, vis_fn, n_chunks, tq, topk)

        q4 = _stack_heads(q_ref[0, pl.ds(row0, tq), :])

        def attend(j, carry):
            k0 = pl.multiple_of(j * CHUNK, CHUNK)
            bias = bias_scr[j, 0:tq, :]
            s = _dot_nt(q4, kc_scr[pl.ds(k0, CHUNK), :]) * scale + jnp.concatenate([bias] * N_HEADS, axis=0)
            return _masked_softmax_step(carry, s, vc_scr[pl.ds(k0, CHUNK), :])
        _, l, acc = lax.fori_loop(0, n_chunks, attend, _softmax_init(N_HEADS * tq, HEAD_DIM))
        out = acc / l
        for h in range(N_HEADS):
            o_ref[0, pl.ds(row0, tq), h * HEAD_DIM:(h + 1) * HEAD_DIM] = out[h * tq:(h + 1) * tq].astype(o_ref.dtype)

    def body(i, _):
        tile(pl.multiple_of(i * CHUNK, CHUNK), CHUNK, i)
        return 0
    lax.fori_loop(0, n_full, body, 0)
    if tail:
        tile(n_full * CHUNK, tail, n_full)


def dsa_prompt(proj32, proj16, topk):
    b, t, _ = proj16.shape
    n_chunks = pl.cdiv(t, CHUNK)
    tp = n_chunks * CHUNK
    wide = lambda off: pl.BlockSpec((1, t, MIX_W), lambda i, o=off // MIX_W: (i, 0, o))
    narrow = lambda off: pl.BlockSpec((1, t, LANES), lambda i, o=off // LANES: (i, 0, o))
    return pl.pallas_call(
        functools.partial(_dsa_prompt_kernel, t=t, topk=topk),
        grid=(b,),
        in_specs=[wide(OFF_QC), wide(OFF_QI), narrow(OFF_KC), narrow(OFF_VC), narrow(OFF_MISC), narrow(OFF_MISC)],
        out_specs=pl.BlockSpec((1, t, MIX_W), lambda i: (i, 0, 0)),
        out_shape=jax.ShapeDtypeStruct((b, t, MIX_W), BF16),
        scratch_shapes=[pltpu.VMEM((tp, HEAD_DIM), BF16), pltpu.VMEM((tp, HEAD_DIM), BF16),
                        pltpu.VMEM((tp, LANES), BF16),
                        pltpu.VMEM((n_chunks, CHUNK, CHUNK), I32), pltpu.VMEM((n_chunks, CHUNK, CHUNK), F32)],
        compiler_params=_cparams("parallel"),
        name="dsa_prompt",
    )(proj16, proj16, proj16, proj16, proj32, proj16)


CONV_PAD = 8


def _rglru_kernel(xd_ref, cs_ref, h0_ref, cw_ref, cb_ref, wa_ref, ba_ref, wx_ref, bx_ref, lam_ref,
                  y_ref, hl_ref, xpad_scr, a_scr, u_scr, hs_scr, *, t):
    n_full, tail = _query_tiles(t)
    xpad_scr[pl.ds(0, CONV_PAD), :] = jnp.zeros((CONV_PAD, MIX_W), F32)
    xpad_scr[pl.ds(CONV_PAD - (CONV_W - 1), CONV_W - 1), :] = cs_ref[0]
    xpad_scr[pl.ds(CONV_PAD, t), :] = xd_ref[0]
    decay_rate = -RG_C * _softplus(-lam_ref[...])

    def chunk(r0, rows, h):
        win = xpad_scr[pl.ds(r0, rows + CONV_PAD), :]
        xc = cb_ref[...]
        for i in range(CONV_W):
            lo = CONV_PAD - (CONV_W - 1) + i
            xc = xc + win[lo:lo + rows] * cw_ref[i:i + 1, :]
        xc16 = xc.astype(BF16)
        gate_r = jax.nn.sigmoid(jnp.dot(xc16, wa_ref[...], preferred_element_type=F32) + ba_ref[...])
        gate_i = jax.nn.sigmoid(jnp.dot(xc16, wx_ref[...], preferred_element_type=F32) + bx_ref[...])
        log_a = gate_r * decay_rate
        a_scr[0:rows, :] = jnp.exp(log_a)
        th = jnp.tanh(log_a)
        u_scr[0:rows, :] = jnp.sqrt(-2.0 * th / (1.0 - th)) * gate_i * xc

        def step(r, hh):
            hh = a_scr[pl.ds(r, 1), :] * hh + u_scr[pl.ds(r, 1), :]
            hs_scr[pl.ds(r, 1), :] = hh
            return hh
        h = lax.fori_loop(0, rows, step, h, unroll=8)
        y_ref[0, pl.ds(r0, rows), :] = hs_scr[0:rows, :].astype(y_ref.dtype)
        return h

    h = h0_ref[0]
    if n_full:
        h = lax.fori_loop(0, n_full, lambda i, hh: chunk(pl.multiple_of(i * CHUNK, CHUNK), CHUNK, hh), h)
    if tail:
        h = chunk(n_full * CHUNK, tail, h)
    hl_ref[0] = h


def rglru(proj32, conv_state, h0, lw, out_dtype):
    b, t, _ = proj32.shape
    conv_w, conv_b, wa16, b_a, wx16, b_x, lam = (lw[k] for k in ("conv_w", "conv_b", "w_a", "b_a", "w_x", "b_x", "lam"))
    row = lambda v: v.reshape(1, MIX_W)
    const = lambda shape: pl.BlockSpec(shape, lambda i: (0,) * len(shape))
    return pl.pallas_call(
        functools.partial(_rglru_kernel, t=t),
        grid=(b,),
        in_specs=[pl.BlockSpec((1, t, MIX_W), lambda i: (i, 0, OFF_XD // MIX_W)),
                  pl.BlockSpec((1, CONV_W - 1, MIX_W), lambda i: (i, 0, 0)),
                  pl.BlockSpec((1, 1, MIX_W), lambda i: (i, 0, 0)),
                  const((CONV_W, MIX_W)), const((1, MIX_W)),
                  const((MIX_W, MIX_W)), const((1, MIX_W)),
                  const((MIX_W, MIX_W)), const((1, MIX_W)), const((1, MIX_W))],
        out_specs=[pl.BlockSpec((1, t, MIX_W), lambda i: (i, 0, 0)),
                   pl.BlockSpec((1, 1, MIX_W), lambda i: (i, 0, 0))],
        out_shape=[jax.ShapeDtypeStruct((b, t, MIX_W), out_dtype), jax.ShapeDtypeStruct((b, 1, MIX_W), F32)],
        scratch_shapes=[pltpu.VMEM((CONV_PAD + pl.cdiv(t, CHUNK) * CHUNK + CONV_PAD, MIX_W), F32),
                        pltpu.VMEM((CHUNK, MIX_W), F32), pltpu.VMEM((CHUNK, MIX_W), F32),
                        pltpu.VMEM((CHUNK, MIX_W), F32)],
        compiler_params=_cparams("parallel"),
        name="rglru",
    )(proj32, conv_state, h0.reshape(b, 1, MIX_W), conv_w, row(conv_b), wa16, row(b_a), wx16, row(b_x), row(lam))


def _block_diag(w):
    n, c, e = w.shape
    eye = jnp.eye(n, dtype=w.dtype)
    return (eye[:, None, :, None] * w[:, :, None, :]).reshape(n * c, n * e)


def _merge_kernel(h_ref, ya_ref, yb_ref, yc_ref, yd_ref, g0_ref, g1_ref, g2_ref, g3_ref, wb_ref, o_ref):
    h = h_ref[...]
    acc = None
    for n, (y_ref, g_ref) in enumerate(((ya_ref, g0_ref), (yb_ref, g1_ref), (yc_ref, g2_ref), (yd_ref, g3_ref))):
        gate = jax.nn.sigmoid(jnp.dot(h, g_ref[...], preferred_element_type=F32))
        term = gate * jnp.dot(y_ref[...], wb_ref[n], preferred_element_type=F32)
        acc = term if acc is None else acc + term
    o_ref[...] = acc.astype(o_ref.dtype)


def merge(h16, ys, wg16, wb16, tm_pref=512, tn=512):
    m, d = h16.shape
    tm = _row_tile(m, tm_pref)
    nj = d // tn
    y_spec = pl.BlockSpec((tm, MIX_W), lambda i, j: (i, 0))
    g_spec = lambda n: pl.BlockSpec((d, tn), lambda i, j, n=n: (0, n * nj + j))
    return pl.pallas_call(
        _merge_kernel,
        grid=(m // tm, nj),
        in_specs=[pl.BlockSpec((tm, d), lambda i, j: (i, 0)), y_spec, y_spec, y_spec, y_spec,
                  g_spec(0), g_spec(1), g_spec(2), g_spec(3),
                  pl.BlockSpec((4, MIX_W, tn), lambda i, j: (0, 0, j))],
        out_specs=pl.BlockSpec((tm, tn), lambda i, j: (i, j)),
        out_shape=jax.ShapeDtypeStruct((m, d), BF16),
        compiler_params=_cparams("parallel", "arbitrary"),
        name="merge",
    )(h16, *ys, wg16, wg16, wg16, wg16, wb16)


def _matmul_res_kernel(a_ref, w_ref, x_ref, o_ref):
    o_ref[...] = x_ref[...] + jnp.dot(a_ref[...], w_ref[...], preferred_element_type=F32)


def matmul_residual(a16, w16, x, tm_pref=512, tn=512):
    m, k = a16.shape
    n = w16.shape[1]
    tm = _row_tile(m, tm_pref)
    return pl.pallas_call(
        _matmul_res_kernel,
        grid=(m // tm, n // tn),
        in_specs=[pl.BlockSpec((tm, k), lambda i, j: (i, 0)),
                  pl.BlockSpec((k, tn), lambda i, j: (0, j)),
                  pl.BlockSpec((tm, tn), lambda i, j: (i, j))],
        out_specs=pl.BlockSpec((tm, tn), lambda i, j: (i, j)),
        out_shape=jax.ShapeDtypeStruct((m, n), F32),
        compiler_params=_cparams("parallel", "arbitrary"),
        name="out_proj",
    )(a16, w16, x)


def _ffn_kernel(x_ref, g_ref, wg_ref, wu_ref, wd_ref, o_ref, h_scr, acc_scr):
    f = pl.program_id(1)

    @pl.when(f == 0)
    def _():
        x = x_ref[...]
        y = x * lax.rsqrt(jnp.mean(x * x, axis=-1, keepdims=True) + RMS_EPS)
        h_scr[...] = (y * g_ref[...]).astype(BF16)
        acc_scr[...] = jnp.zeros_like(acc_scr)

    h = h_scr[...]
    gate = jnp.dot(h, wg_ref[...], preferred_element_type=F32)
    up = jnp.dot(h, wu_ref[...], preferred_element_type=F32)
    act = (jax.nn.silu(gate) * up).astype(BF16)
    acc_scr[...] += jnp.dot(act, wd_ref[...], preferred_element_type=F32)

    @pl.when(f == pl.num_programs(1) - 1)
    def _():
        o_ref[...] = x_ref[...] + acc_scr[...]


def ffn(x, g, wg16, wu16, wd16, tm_pref=512, tf=512):
    m, d = x.shape
    dff = wg16.shape[1]
    tm = _row_tile(m, tm_pref)
    return pl.pallas_call(
        _ffn_kernel,
        grid=(m // tm, dff // tf),
        in_specs=[pl.BlockSpec((tm, d), lambda i, f: (i, 0)),
                  pl.BlockSpec((1, d), lambda i, f: (0, 0)),
                  pl.BlockSpec((d, tf), lambda i, f: (0, f)),
                  pl.BlockSpec((d, tf), lambda i, f: (0, f)),
                  pl.BlockSpec((tf, d), lambda i, f: (f, 0))],
        out_specs=pl.BlockSpec((tm, d), lambda i, f: (i, 0)),
        out_shape=jax.ShapeDtypeStruct((m, d), F32),
        scratch_shapes=[pltpu.VMEM((tm, d), BF16), pltpu.VMEM((tm, d), F32)],
        compiler_params=_cparams("parallel", "arbitrary"),
        name="ffn",
    )(x, g.reshape(1, d), wg16, wu16, wd16)


def _rmsnorm_kernel(x_ref, g_ref, o_ref):
    x = x_ref[...]
    o_ref[...] = x * lax.rsqrt(jnp.mean(x * x, axis=-1, keepdims=True) + RMS_EPS) * g_ref[...]


def rmsnorm(x, g, tm_pref=512):
    m, d = x.shape
    tm = _row_tile(m, tm_pref)
    return pl.pallas_call(
        _rmsnorm_kernel,
        grid=(m // tm,),
        in_specs=[pl.BlockSpec((tm, d), lambda i: (i, 0)), pl.BlockSpec((1, d), lambda i: (0, 0))],
        out_specs=pl.BlockSpec((tm, d), lambda i: (i, 0)),
        out_shape=jax.ShapeDtypeStruct((m, d), F32),
        compiler_params=_cparams("parallel"),
        name="final_norm",
    )(x, g.reshape(1, d))


PAGES_PER_STEP = 8


def _page_specs(width, layer, n_pages, reverse, rows=PAGE):
    def spec(p):
        def index(b, s, pt):
            j = s * PAGES_PER_STEP + p
            if reverse:
                j = n_pages - 1 - j
            return (layer, pt[b, j], 0, 0)
        return pl.BlockSpec((1, 1, rows, width), index)
    return [spec(p) for p in range(PAGES_PER_STEP)]


def _per_seq(shape):
    return pl.BlockSpec((1,) + shape, lambda b, s, pt: (b,) + (0,) * len(shape))


def _block_diag_queries(q):
    col = _iota(q.shape, 1)
    parts = [jnp.where((col >= h * HEAD_DIM) & (col < (h + 1) * HEAD_DIM), q, 0.0) for h in range(N_HEADS)]
    return jnp.concatenate(parts, axis=0).astype(BF16)


def _query_index(t):
    return jnp.concatenate([_iota((t, CHUNK), 0)] * N_HEADS, axis=0)


def _pad_rows(dst, src):
    dst[...] = jnp.zeros(dst.shape, dst.dtype)
    dst[pl.ds(0, src.shape[0]), :] = src


def _per_head_rows(x, t):
    return jnp.concatenate([jnp.broadcast_to(x[h:h + 1], (t, x.shape[1])) for h in range(N_HEADS)], axis=0)


def _per_head_cols(x):
    return jnp.concatenate([x[:, h:h + 1] for h in range(N_HEADS)], axis=0)


def _write_heads(o_ref, out, t):
    for h in range(N_HEADS):
        hs = slice(h * HEAD_DIM, (h + 1) * HEAD_DIM)
        o_ref[0, :, hs] = out[h * t:(h + 1) * t, hs].astype(o_ref.dtype)


def _fox_sample_kernel(pt_ref, q_ref, kn_ref, vn_ref, cn_ref, cnrow_ref, *rest, t):
    n = PAGES_PER_STEP
    k_refs, v_refs, lf_refs = rest[0:n], rest[n:2 * n], rest[2 * n:3 * n]
    o_ref, kpad, vpad, m_scr, l_scr, acc_scr, d_scr = rest[3 * n:]
    s_id = pl.program_id(1)
    scale = HEAD_DIM ** -0.5
    rows = N_HEADS * t
    qbd = _block_diag_queries(q_ref[0])
    cn = _per_head_cols(cn_ref[0])

    @pl.when(s_id == 0)
    def _():
        _pad_rows(kpad, kn_ref[0])
        _pad_rows(vpad, vn_ref[0])
        s = _dot_nt(qbd, kpad[...].astype(BF16)) * scale + cn - _per_head_rows(cnrow_ref[0], t)
        col = _iota((rows, CHUNK), 1)
        s = jnp.where((col <= _query_index(t)) & (col < t), s, NEG_INF)
        m, l, acc = _masked_softmax_step(_softmax_init(rows, MIX_W), s, vpad[...].astype(BF16))
        m_scr[...], l_scr[...], acc_scr[...] = m, l, acc
        d_scr[...] = jnp.zeros(d_scr.shape, F32)

    suffix = _suffix_matrix()
    carry = (m_scr[...], l_scr[...], acc_scr[...])
    later = d_scr[...]
    for p in range(n):
        lf = jnp.concatenate([lf_refs[p][0, 0], jnp.zeros((SUBLANES - N_HEADS, PAGE), F32)], axis=0)
        decay = _dot01_right(lf, suffix) + later
        later = later + jnp.sum(lf, axis=1, keepdims=True)
        s = _dot_nt(qbd, k_refs[p][0, 0].astype(BF16)) * scale + cn + _per_head_rows(decay[0:N_HEADS], t)
        carry = _masked_softmax_step(carry, s, v_refs[p][0, 0].astype(BF16))
    m_scr[...], l_scr[...], acc_scr[...] = carry
    d_scr[...] = later

    @pl.when(s_id == pl.num_programs(1) - 1)
    def _():
        _write_heads(o_ref, carry[2] / carry[1], t)


def fox_sample(page_table, layer, q, kn, vn, cn, cnrow, cache_k, cache_v, cache_lft):
    b, t, _ = q.shape
    n_pages = page_table.shape[1]
    rows = N_HEADS * t
    grid_spec = pltpu.PrefetchScalarGridSpec(
        num_scalar_prefetch=1,
        grid=(b, n_pages // PAGES_PER_STEP),
        in_specs=[_per_seq((t, MIX_W)), _per_seq((t, MIX_W)), _per_seq((t, MIX_W)),
                  _per_seq((t, N_HEADS)), _per_seq((N_HEADS, CHUNK))]
                 + _page_specs(MIX_W, layer, n_pages, True) + _page_specs(MIX_W, layer, n_pages, True)
                 + _page_specs(PAGE, layer, n_pages, True, rows=N_HEADS),
        out_specs=_per_seq((t, MIX_W)),
        scratch_shapes=[pltpu.VMEM((CHUNK, MIX_W), F32), pltpu.VMEM((CHUNK, MIX_W), F32),
                        pltpu.VMEM((rows, 1), F32), pltpu.VMEM((rows, 1), F32), pltpu.VMEM((rows, MIX_W), F32),
                        pltpu.VMEM((SUBLANES, 1), F32)])
    return pl.pallas_call(
        functools.partial(_fox_sample_kernel, t=t),
        grid_spec=grid_spec,
        out_shape=jax.ShapeDtypeStruct((b, t, MIX_W), F32),
        compiler_params=_cparams("parallel", "arbitrary"),
        name="fox_sample",
    )(page_table, q, kn, vn, cn, cnrow, *([cache_k] * PAGES_PER_STEP), *([cache_v] * PAGES_PER_STEP),
      *([cache_lft] * PAGES_PER_STEP))


def _sb_sample_kernel(pt_ref, q_ref, kn_ref, vn_ref, *rest, t):
    n = PAGES_PER_STEP
    k_refs, v_refs = rest[0:n], rest[n:2 * n]
    o_ref, kpad, vpad, r_scr, acc_scr = rest[2 * n:]
    s_id = pl.program_id(1)
    scale = HEAD_DIM ** -0.5
    rows = N_HEADS * t
    qbd = _block_diag_queries(q_ref[0])
    suffix = _suffix_matrix()

    @pl.when(s_id == 0)
    def _():
        _pad_rows(kpad, kn_ref[0])
        _pad_rows(vpad, vn_ref[0])
        z = _dot_nt(qbd, kpad[...].astype(BF16)) * scale
        vis = _iota((rows, CHUNK), 1) < _query_index(t)
        log_keep = jnp.where(vis, -_softplus(z), 0.0)
        att = jnp.where(vis, jnp.exp(log_keep + z + _dot01_right(log_keep, suffix)), 0.0)
        acc_scr[...] = jnp.dot(att.astype(BF16), vpad[...].astype(BF16), preferred_element_type=F32)
        r_scr[...] = jnp.sum(log_keep, axis=1, keepdims=True)

    later, acc = r_scr[...], acc_scr[...]
    for p in range(n):
        z = _dot_nt(qbd, k_refs[p][0, 0].astype(BF16)) * scale
        log_keep = -_softplus(z)
        att = jnp.exp(log_keep + z + _dot01_right(log_keep, suffix) + later)
        acc = acc + jnp.dot(att.astype(BF16), v_refs[p][0, 0].astype(BF16), preferred_element_type=F32)
        later = later + jnp.sum(log_keep, axis=1, keepdims=True)
    r_scr[...], acc_scr[...] = later, acc

    @pl.when(s_id == pl.num_programs(1) - 1)
    def _():
        _write_heads(o_ref, acc, t)


def sb_sample(page_table, layer, q, kn, vn, cache_k, cache_v):
    b, t, _ = q.shape
    n_pages = page_table.shape[1]
    rows = N_HEADS * t
    grid_spec = pltpu.PrefetchScalarGridSpec(
        num_scalar_prefetch=1,
        grid=(b, n_pages // PAGES_PER_STEP),
        in_specs=[_per_seq((t, MIX_W))] * 3
                 + _page_specs(MIX_W, layer, n_pages, True) + _page_specs(MIX_W, layer, n_pages, True),
        out_specs=_per_seq((t, MIX_W)),
        scratch_shapes=[pltpu.VMEM((CHUNK, MIX_W), F32), pltpu.VMEM((CHUNK, MIX_W), F32),
                        pltpu.VMEM((rows, 1), F32), pltpu.VMEM((rows, MIX_W), F32)])
    return pl.pallas_call(
        functools.partial(_sb_sample_kernel, t=t),
        grid_spec=grid_spec,
        out_shape=jax.ShapeDtypeStruct((b, t, MIX_W), F32),
        compiler_params=_cparams("parallel", "arbitrary"),
        name="sb_sample",
    )(page_table, q, kn, vn, *([cache_k] * PAGES_PER_STEP), *([cache_v] * PAGES_PER_STEP))


def _dsa_scores_kernel(pt_ref, qi_ref, misc_ref, *rest):
    n = PAGES_PER_STEP
    idx_refs = rest[0:n]
    past_ref, new_ref, kpad = rest[n:]
    qi = qi_ref[0].astype(BF16)
    wi = misc_ref[0, :, MISC_WI:MISC_WI + IDX_HEADS]

    @pl.when(pl.program_id(1) == 0)
    def _():
        _pad_rows(kpad, misc_ref[0])
        new_ref[0] = _indexer_scores(qi, wi, kpad[:, MISC_KI:MISC_KI + IDX_DIM].astype(BF16))

    for p in range(n):
        past_ref[0, p] = _indexer_scores(qi, wi, idx_refs[p][0, 0].astype(BF16))


def dsa_sample_scores(page_table, layer, qi, misc, cache_idx):
    b, t, _ = qi.shape
    n_pages = page_table.shape[1]
    grid_spec = pltpu.PrefetchScalarGridSpec(
        num_scalar_prefetch=1,
        grid=(b, n_pages // PAGES_PER_STEP),
        in_specs=[_per_seq((t, MIX_W)), _per_seq((t, LANES))] + _page_specs(IDX_DIM, layer, n_pages, False),
        out_specs=[pl.BlockSpec((1, PAGES_PER_STEP, t, PAGE), lambda i, s, pt: (i, s, 0, 0)),
                   _per_seq((t, CHUNK))],
        scratch_shapes=[pltpu.VMEM((CHUNK, LANES), F32)])
    return pl.pallas_call(
        _dsa_scores_kernel,
        grid_spec=grid_spec,
        out_shape=[jax.ShapeDtypeStruct((b, n_pages, t, PAGE), F32), jax.ShapeDtypeStruct((b, t, CHUNK), F32)],
        compiler_params=_cparams("parallel", "arbitrary"),
        name="dsa_sample_scores",
    )(page_table, qi, misc, *([cache_idx] * PAGES_PER_STEP))


def _dsa_select_kernel(past_ref, new_ref, bias_past_ref, bias_new_ref, key_scr, bias_scr, *, t, n_pages, topk):
    def vis_fn(j):
        col = _iota((t, CHUNK), 1)
        new = (col <= _iota((t, CHUNK), 0)) & (col < t)
        return jnp.logical_or(j < n_pages, new)

    def fill(j, _):
        key_scr[j] = _order_key(past_ref[0, j])
        return 0
    lax.fori_loop(0, n_pages, fill, 0)
    key_scr[n_pages] = jnp.where(vis_fn(n_pages), _order_key(new_ref[0]), INT_MIN)
    _select_topk(key_scr, bias_scr, vis_fn, n_pages + 1, t, topk)

    def emit(j, _):
        bias_past_ref[0, j] = bias_scr[j]
        return 0
    lax.fori_loop(0, n_pages, emit, 0)
    bias_new_ref[0] = bias_scr[n_pages]


def dsa_sample_select(score_past, score_new, topk):
    b, n_pages, t, _ = score_past.shape
    past = pl.BlockSpec((1, n_pages, t, PAGE), lambda i: (i, 0, 0, 0))
    new = pl.BlockSpec((1, t, CHUNK), lambda i: (i, 0, 0))
    return pl.pallas_call(
        functools.partial(_dsa_select_kernel, t=t, n_pages=n_pages, topk=topk),
        grid=(b,),
        in_specs=[past, new],
        out_specs=[past, new],
        out_shape=[jax.ShapeDtypeStruct(score_past.shape, F32), jax.ShapeDtypeStruct(score_new.shape, F32)],
        scratch_shapes=[pltpu.VMEM((n_pages + 1, t, CHUNK), I32), pltpu.VMEM((n_pages + 1, t, CHUNK), F32)],
        compiler_params=_cparams("parallel"),
        name="dsa_sample_select",
    )(score_past, score_new)


def _dsa_attend_kernel(pt_ref, q_ref, kn_ref, vn_ref, bias_new_ref, bias_past_ref, *rest, t):
    n = PAGES_PER_STEP
    k_refs, v_refs = rest[0:n], rest[n:2 * n]
    o_ref, kpad, vpad, m_scr, l_scr, acc_scr = rest[2 * n:]
    s_id = pl.program_id(1)
    scale = HEAD_DIM ** -0.5
    rows = N_HEADS * t
    q4 = _stack_heads(q_ref[0]).astype(BF16)

    @pl.when(s_id == 0)
    def _():
        _pad_rows(kpad, kn_ref[0])
        _pad_rows(vpad, vn_ref[0])
        s = _dot_nt(q4, kpad[...].astype(BF16)) * scale + jnp.concatenate([bias_new_ref[0]] * N_HEADS, axis=0)
        m, l, acc = _masked_softmax_step(_softmax_init(rows, HEAD_DIM), s, vpad[...].astype(BF16))
        m_scr[...], l_scr[...], acc_scr[...] = m, l, acc

    carry = (m_scr[...], l_scr[...], acc_scr[...])
    for p in range(n):
        bias = jnp.concatenate([bias_past_ref[0, p]] * N_HEADS, axis=0)
        s = _dot_nt(q4, k_refs[p][0, 0].astype(BF16)) * scale + bias
        carry = _masked_softmax_step(carry, s, v_refs[p][0, 0].astype(BF16))
    m_scr[...], l_scr[...], acc_scr[...] = carry

    @pl.when(s_id == pl.num_programs(1) - 1)
    def _():
        out = carry[2] / carry[1]
        for h in range(N_HEADS):
            o_ref[0, :, h * HEAD_DIM:(h + 1) * HEAD_DIM] = out[h * t:(h + 1) * t].astype(o_ref.dtype)


def dsa_sample_attend(page_table, layer, q, kn, vn, bias_new, bias_past, cache_k, cache_v):
    b, t, _ = q.shape
    n_pages = page_table.shape[1]
    rows = N_HEADS * t
    grid_spec = pltpu.PrefetchScalarGridSpec(
        num_scalar_prefetch=1,
        grid=(b, n_pages // PAGES_PER_STEP),
        in_specs=[_per_seq((t, MIX_W)), _per_seq((t, HEAD_DIM)), _per_seq((t, HEAD_DIM)), _per_seq((t, CHUNK)),
                  pl.BlockSpec((1, PAGES_PER_STEP, t, PAGE), lambda i, s, pt: (i, s, 0, 0))]
                 + _page_specs(HEAD_DIM, layer, n_pages, False) + _page_specs(HEAD_DIM, layer, n_pages, False),
        out_specs=_per_seq((t, MIX_W)),
        scratch_shapes=[pltpu.VMEM((CHUNK, HEAD_DIM), F32), pltpu.VMEM((CHUNK, HEAD_DIM), F32),
                        pltpu.VMEM((rows, 1), F32), pltpu.VMEM((rows, 1), F32), pltpu.VMEM((rows, HEAD_DIM), F32)])
    return pl.pallas_call(
        functools.partial(_dsa_attend_kernel, t=t),
        grid_spec=grid_spec,
        out_shape=jax.ShapeDtypeStruct((b, t, MIX_W), F32),
        compiler_params=_cparams("parallel", "arbitrary"),
        name="dsa_sample_attend",
    )(page_table, q, kn, vn, bias_new, bias_past, *([cache_k] * PAGES_PER_STEP), *([cache_v] * PAGES_PER_STEP))


def _pad_w_in(w):
    widths = (512, 512, 512, 4, 512, 512, 512, 512, 128, 128, 512, 64, 8, 512)
    names = ("qa", "ka", "va", "fa", "qb", "kb", "vb", "qc", "kc", "vc", "qi", "ki", "wi", "xd")
    seg, off = {}, 0
    for name, width in zip(names, widths):
        seg[name] = w[:, off:off + width]
        off += width
    order = ("qa", "ka", "va", "qb", "kb", "vb", "qc", "qi", "xd", "kc", "vc", "ki", "fa", "wi")
    cols = jnp.concatenate([seg[n] for n in order], axis=1)
    return jnp.pad(cols, ((0, 0), (0, N_PROJ - cols.shape[1]))).astype(BF16)


def _cols(p, off, width):
    return p[..., off:off + width]


def _new_rows(p32, lf, b, t):
    heads = lambda off: _cols(p32, off, MIX_W).reshape(b, t, N_HEADS, HEAD_DIM)
    return (heads(OFF_KA), heads(OFF_VA), _cols(lf, MISC_FA, N_HEADS), heads(OFF_KB), heads(OFF_VB),
            _cols(p32, OFF_KC, HEAD_DIM), _cols(p32, OFF_VC, HEAD_DIM), _cols(p32, OFF_MISC + MISC_KI, IDX_DIM))


def _dense_tail(x2d, h16, ys, lw):
    merged = merge(h16, ys, lw["w_gate"], lw["w_branch"])
    x2d = matmul_residual(merged, lw["w_out"], x2d)
    return ffn(x2d, lw["g_ffn"], lw["w_fg"], lw["w_fu"], lw["w_fd"])


def _forget_bias_row(b_f):
    return jnp.zeros((1, LANES), F32).at[0, MISC_FA:MISC_FA + N_HEADS].set(b_f)


def _conv_tail(conv_state, p32):
    xd = _cols(p32, OFF_XD, MIX_W)
    return jnp.concatenate([conv_state, xd], axis=1)[:, -(CONV_W - 1):]


def _prompt_layer(x2d, b, t, lw, topk):
    p32, p16, h16 = norm_proj(x2d, lw["g_mix"], lw["w_in"])
    p32, p16 = p32.reshape(b, t, N_PROJ), p16.reshape(b, t, N_PROJ)
    lf, fc = logf_cumsum(_cols(p32, OFF_MISC, LANES), _forget_bias_row(lw["b_f"]), jnp.zeros((b, 1, LANES), F32))
    fcol = _cols(fc, MISC_FA, N_HEADS)
    conv0 = jnp.zeros((b, CONV_W - 1, MIX_W), F32)
    ya = fox_prompt(p16, fcol, _frow_chunks(fcol))
    yb = sb_prompt(p16)
    yc = dsa_prompt(p32, p16, topk)
    yd, h_last = rglru(p32, conv0, jnp.zeros((b, MIX_W), F32), lw, BF16)
    ys = [y.reshape(b * t, MIX_W) for y in (ya, yb, yc, yd)]
    rows = _new_rows(p32, lf, b, t) + (_conv_tail(conv0, p32), h_last.reshape(b, MIX_W))
    return _dense_tail(x2d, h16, ys, lw), rows


def _sample_layer(x2d, b, t, lw, topk, layer, page_table, caches, conv_state, h0):
    ca_k, ca_v, ca_lft, cb_k, cb_v, cc_k, cc_v, cc_idx = caches
    p32, _, h16 = norm_proj(x2d, lw["g_mix"], lw["w_in"])
    p32 = p32.reshape(b, t, N_PROJ)
    misc = _cols(p32, OFF_MISC, LANES)
    lf, fc = logf_cumsum(misc, _forget_bias_row(lw["b_f"]), jnp.zeros((b, 1, LANES), F32))
    cn = _cols(fc, MISC_FA, N_HEADS)
    cnrow = jnp.pad(cn.transpose(0, 2, 1), ((0, 0), (0, 0), (0, CHUNK - t)))
    ya = fox_sample(page_table, layer, _cols(p32, OFF_QA, MIX_W), _cols(p32, OFF_KA, MIX_W), _cols(p32, OFF_VA, MIX_W),
                    cn, cnrow, ca_k, ca_v, ca_lft)
    yb = sb_sample(page_table, layer, _cols(p32, OFF_QB, MIX_W), _cols(p32, OFF_KB, MIX_W), _cols(p32, OFF_VB, MIX_W),
                   cb_k, cb_v)
    score_past, score_new = dsa_sample_scores(page_table, layer, _cols(p32, OFF_QI, MIX_W), misc, cc_idx)
    bias_past, bias_new = dsa_sample_select(score_past, score_new, topk)
    yc = dsa_sample_attend(page_table, layer, _cols(p32, OFF_QC, MIX_W), _cols(p32, OFF_KC, HEAD_DIM),
                           _cols(p32, OFF_VC, HEAD_DIM), bias_new, bias_past, cc_k, cc_v)
    yd, h_last = rglru(p32, conv_state, h0, lw, F32)
    ys = [y.reshape(b * t, MIX_W).astype(BF16) for y in (ya, yb, yc, yd)]
    rows = _new_rows(p32, lf, b, t) + (_conv_tail(conv_state, p32), h_last.reshape(b, MIX_W))
    return _dense_tail(x2d, h16, ys, lw), rows


def kernel(x_prompt, x_sample, cache_a_k, cache_a_v, cache_a_logf, cache_b_k, cache_b_v, cache_c_k, cache_c_v, cache_c_idx_k, state_d_conv, state_d_h, page_table, meta_tokens, w_in, b_forget, conv_w, conv_b, w_rg_a, b_rg_a, w_rg_x, b_rg_x, rg_lambda, w_gate, w_branch, w_out, norm_mix, norm_ffn, w_ffn_gate, w_ffn_up, w_ffn_down, norm_final):
    depth = w_in.shape[0]
    n_p, seq_p, _ = x_prompt.shape
    n_dec, t_dec, _ = x_sample.shape
    n_pool = cache_a_k.shape[1]
    past_len = page_table.shape[1] * PAGE
    topk_p = min(TOPK_MAX, seq_p // 4)
    topk_s = min(TOPK_MAX, (past_len + t_dec) // 4)

    flat = lambda c: c.reshape(depth, n_pool, PAGE, -1)
    caches = (flat(cache_a_k), flat(cache_a_v), cache_a_logf.transpose(0, 1, 3, 2),
              flat(cache_b_k), flat(cache_b_v), cache_c_k, cache_c_v, cache_c_idx_k)

    xp = jnp.concatenate([jnp.broadcast_to(meta_tokens[None], (n_p, N_META, D_MODEL)), x_prompt], axis=1)
    t_p = seq_p + N_META
    xp = xp.reshape(n_p * t_p, D_MODEL)
    xs = x_sample.reshape(n_dec * t_dec, D_MODEL)

    prompt_rows, sample_rows = [], []
    for l in range(depth):
        lw = dict(w_in=_pad_w_in(w_in[l]), b_f=b_forget[l], conv_w=conv_w[l], conv_b=conv_b[l],
                  w_a=_block_diag(w_rg_a[l]).astype(BF16), b_a=b_rg_a[l],
                  w_x=_block_diag(w_rg_x[l]).astype(BF16), b_x=b_rg_x[l], lam=rg_lambda[l],
                  w_gate=w_gate[l].astype(BF16), w_branch=w_branch[l].astype(BF16), w_out=w_out[l].astype(BF16),
                  g_mix=norm_mix[l], g_ffn=norm_ffn[l], w_fg=w_ffn_gate[l].astype(BF16),
                  w_fu=w_ffn_up[l].astype(BF16), w_fd=w_ffn_down[l].astype(BF16))
        xp, rows = _prompt_layer(xp, n_p, t_p, lw, topk_p)
        prompt_rows.append(rows)
        xs, rows = _sample_layer(xs, n_dec, t_dec, lw, topk_s, l, page_table, caches, state_d_conv[l], state_d_h[l])
        sample_rows.append(rows)

    y_prompt = rmsnorm(xp, norm_final).reshape(n_p, t_p, D_MODEL)[:, N_META:]
    y_sample = rmsnorm(xs, norm_final).reshape(n_dec, t_dec, D_MODEL)
    stack = lambda rows: tuple(jnp.stack(list(r)) for r in zip(*rows))
    return (y_prompt, y_sample) + stack(prompt_rows) + stack(sample_rows)
```

```python
import functools

import jax
import jax.numpy as jnp
from jax import lax
from jax.experimental import pallas as pl
from jax.experimental.pallas import tpu as pltpu

F32 = jnp.float32
BF16 = jnp.bfloat16
I32 = jnp.int32

D_MODEL = 2048
N_META = 16
HEAD_DIM = 128
N_HEADS = 4
MIX_W = 512
IDX_HEADS = 8
IDX_DIM = 64
TOPK_MAX = 256
RG_C = 8.0
CONV_W = 4
RMS_EPS = 1e-6
PAGE = 128

LANES = 128
SUBLANES = 8
CHUNK = 128
BAND_TILES = 4
VMEM_LIMIT = 56 * 1024 * 1024

OFF_QA, OFF_KA, OFF_VA = 0, 512, 1024
OFF_QB, OFF_KB, OFF_VB = 1536, 2048, 2560
OFF_QC, OFF_QI, OFF_XD = 3072, 3584, 4096
OFF_KC, OFF_VC, OFF_MISC = 4608, 4736, 4864
MISC_KI, MISC_FA, MISC_WI = 0, 64, 68
N_PROJ = 5120

NEG_INF = float("-inf")
INT_MIN = -2 ** 31


def _cparams(*sem):
    return pltpu.CompilerParams(dimension_semantics=sem, vmem_limit_bytes=VMEM_LIMIT)


def _split3(x):
    x1 = x.astype(BF16)
    r1 = x - x1.astype(F32)
    x2 = r1.astype(BF16)
    x3 = (r1 - x2.astype(F32)).astype(BF16)
    return x1, x2, x3


def _dot01_left(m01, x):
    return sum(jnp.dot(m01, p, preferred_element_type=F32) for p in _split3(x))


def _dot01_right(x, m01):
    return sum(jnp.dot(p, m01, preferred_element_type=F32) for p in _split3(x))


def _dot_nt(a, b):
    return lax.dot_general(a, b, (((1,), (1,)), ((), ())), preferred_element_type=F32)


def _softplus(x):
    return jnp.maximum(x, 0.0) + jnp.log1p(jnp.exp(-jnp.abs(x)))


def _iota(shape, dim):
    return lax.broadcasted_iota(I32, shape, dim)


def _row_tile(m, pref):
    for t in range(min(pref, m), 7, -1):
        if m % t == 0 and t % SUBLANES == 0:
            return t
    raise ValueError(f"no row tile for {m}")


def _norm_proj_kernel(x_ref, g_ref, w_ref, o32_ref, o16_ref, h16_ref, h_scr):
    @pl.when(pl.program_id(1) == 0)
    def _():
        x = x_ref[...]
        y = x * lax.rsqrt(jnp.mean(x * x, axis=-1, keepdims=True) + RMS_EPS)
        h = (y * g_ref[...]).astype(BF16)
        h_scr[...] = h
        h16_ref[...] = h

    acc = jnp.dot(h_scr[...], w_ref[...], preferred_element_type=F32)
    o32_ref[...] = acc
    o16_ref[...] = acc.astype(BF16)


def norm_proj(x, g, w16, tm_pref=1032, tn=512):
    m, d = x.shape
    n = w16.shape[1]
    tm = _row_tile(m, tm_pref)
    return pl.pallas_call(
        _norm_proj_kernel,
        grid=(m // tm, n // tn),
        in_specs=[pl.BlockSpec((tm, d), lambda i, j: (i, 0)),
                  pl.BlockSpec((1, d), lambda i, j: (0, 0)),
                  pl.BlockSpec((d, tn), lambda i, j: (0, j))],
        out_specs=[pl.BlockSpec((tm, tn), lambda i, j: (i, j)),
                   pl.BlockSpec((tm, tn), lambda i, j: (i, j)),
                   pl.BlockSpec((tm, d), lambda i, j: (i, 0))],
        out_shape=[jax.ShapeDtypeStruct((m, n), F32),
                   jax.ShapeDtypeStruct((m, n), BF16),
                   jax.ShapeDtypeStruct((m, d), BF16)],
        scratch_shapes=[pltpu.VMEM((tm, d), BF16)],
        compiler_params=_cparams("parallel", "arbitrary"),
        name="norm_proj",
    )(x, g.reshape(1, d), w16)


def _logf_kernel(misc_ref, bias_ref, logf_ref, fcum_ref, pad_scr, *, t, n_chunks):
    lane = _iota((1, LANES), 1)
    live = (lane >= MISC_FA) & (lane < MISC_FA + N_HEADS)
    logf = jnp.where(live, jax.nn.log_sigmoid(misc_ref[0] + bias_ref[...]), 0.0)
    logf_ref[0] = logf
    pad_scr[pl.ds(0, t), :] = logf
    if n_chunks * CHUNK > t:
        pad_scr[pl.ds(t, n_chunks * CHUNK - t), :] = jnp.zeros((n_chunks * CHUNK - t, LANES), F32)
    tril = (_iota((CHUNK, CHUNK), 1) <= _iota((CHUNK, CHUNK), 0)).astype(BF16)
    carry = jnp.zeros((1, LANES), F32)
    for c in range(n_chunks):
        f = _dot01_left(tril, pad_scr[pl.ds(c * CHUNK, CHUNK), :]) + carry
        rows = min(CHUNK, t - c * CHUNK)
        fcum_ref[0, pl.ds(c * CHUNK, rows), :] = f[:rows]
        carry = f[CHUNK - 1:CHUNK, :]


def logf_cumsum(misc, bias_row):
    b, t, _ = misc.shape
    n_chunks = pl.cdiv(t, CHUNK)
    return pl.pallas_call(
        functools.partial(_logf_kernel, t=t, n_chunks=n_chunks),
        grid=(b,),
        in_specs=[pl.BlockSpec((1, t, LANES), lambda i: (i, 0, 0)),
                  pl.BlockSpec((1, LANES), lambda i: (0, 0))],
        out_specs=[pl.BlockSpec((1, t, LANES), lambda i: (i, 0, 0)),
                   pl.BlockSpec((1, t, LANES), lambda i: (i, 0, 0))],
        out_shape=[jax.ShapeDtypeStruct((b, t, LANES), F32)] * 2,
        scratch_shapes=[pltpu.VMEM((n_chunks * CHUNK, LANES), F32)],
        compiler_params=_cparams("parallel"),
        name="logf_cumsum",
    )(misc, bias_row)


def _pad_copy(dst, src, t):
    rows = dst.shape[0]
    dst[pl.ds(0, t), :] = src
    if rows > t:
        dst[pl.ds(t, rows - t), :] = jnp.zeros((rows - t, dst.shape[1]), dst.dtype)


def _stage_heads(dst, src_ref, t):
    for h in range(N_HEADS):
        _pad_copy(dst.at[h], src_ref[0, :, h * HEAD_DIM:(h + 1) * HEAD_DIM], t)


def _unstage_heads(o_ref, src, t):
    for h in range(N_HEADS):
        o_ref[0, :, h * HEAD_DIM:(h + 1) * HEAD_DIM] = src[h, 0:t, :]


def _bands(n_tiles):
    return [(lo, min(lo + BAND_TILES, n_tiles)) for lo in range(0, n_tiles, BAND_TILES)]


def _for_heads_and_tiles(n_tiles, tile):
    for lo, hi in _bands(n_tiles):
        def per_head(h, _, lo=lo, hi=hi):
            def per_tile(i, _):
                tile(h, i, hi * CHUNK)
                return 0
            return lax.fori_loop(lo, hi, per_tile, 0)
        lax.fori_loop(0, N_HEADS, per_head, 0)


def _fox_prompt_kernel(q_ref, k_ref, v_ref, frow_ref, o_ref, q_scr, k_scr, v_scr, o_scr, *, t):
    scale = HEAD_DIM ** -0.5
    _stage_heads(q_scr, q_ref, t)
    _stage_heads(k_scr, k_ref, t)
    _stage_heads(v_scr, v_ref, t)

    def tile(h, i, width):
        row0 = pl.multiple_of(i * CHUNK, CHUNK)
        q = q_scr[h, pl.ds(row0, CHUNK), :]
        s = _dot_nt(q, k_scr[h, 0:width, :]) * scale - frow_ref[0, h, :, 0:width]
        vis = _iota((CHUNK, width), 1) <= row0 + _iota((CHUNK, width), 0)
        s = jnp.where(vis, s, NEG_INF)
        p = jnp.exp(s - jnp.max(s, axis=1, keepdims=True))
        l = jnp.sum(p, axis=1, keepdims=True)
        o = jnp.dot(p.astype(BF16), v_scr[h, 0:width, :], preferred_element_type=F32) / l
        o_scr[h, pl.ds(row0, CHUNK), :] = o.astype(o_scr.dtype)

    _for_heads_and_tiles(k_scr.shape[1] // CHUNK, tile)
    _unstage_heads(o_ref, o_scr, t)


def _head_scratch(tp, n):
    return [pltpu.VMEM((N_HEADS, tp, HEAD_DIM), BF16)] * n


def fox_prompt(proj16, frow):
    b, t, _ = proj16.shape
    tp = frow.shape[-1]
    blk = lambda off: pl.BlockSpec((1, t, MIX_W), lambda i, o=off // MIX_W: (i, 0, o))
    return pl.pallas_call(
        functools.partial(_fox_prompt_kernel, t=t),
        grid=(b,),
        in_specs=[blk(OFF_QA), blk(OFF_KA), blk(OFF_VA),
                  pl.BlockSpec((1, N_HEADS, 1, tp), lambda i: (i, 0, 0, 0))],
        out_specs=pl.BlockSpec((1, t, MIX_W), lambda i: (i, 0, 0)),
        out_shape=jax.ShapeDtypeStruct((b, t, MIX_W), BF16),
        scratch_shapes=_head_scratch(tp, 4),
        compiler_params=_cparams("parallel"),
        name="fox_prompt",
    )(proj16, proj16, proj16, frow)


def _suffix_matrix():
    return (_iota((CHUNK, CHUNK), 0) > _iota((CHUNK, CHUNK), 1)).astype(BF16)


def _sb_prompt_kernel(q_ref, k_ref, v_ref, o_ref, q_scr, k_scr, v_scr, o_scr, *, t):
    scale = HEAD_DIM ** -0.5
    _stage_heads(q_scr, q_ref, t)
    _stage_heads(k_scr, k_ref, t)
    _stage_heads(v_scr, v_ref, t)
    suffix = _suffix_matrix()

    def tile(h, i, width):
        row0 = pl.multiple_of(i * CHUNK, CHUNK)
        q = q_scr[h, pl.ds(row0, CHUNK), :]
        z = _dot_nt(q, k_scr[h, 0:width, :]) * scale
        vis = _iota((CHUNK, width), 1) < row0 + _iota((CHUNK, width), 0)
        log_keep = jnp.where(vis, -_softplus(z), 0.0)
        later_chunks = jnp.zeros((CHUNK, 1), F32)
        pieces = [None] * (width // CHUNK)
        for c in reversed(range(width // CHUNK)):
            cs = slice(c * CHUNK, (c + 1) * CHUNK)
            later = _dot01_right(log_keep[:, cs], suffix) + later_chunks
            att = jnp.where(vis[:, cs], jnp.exp(log_keep[:, cs] + z[:, cs] + later), 0.0)
            pieces[c] = att.astype(BF16)
            later_chunks = later_chunks + jnp.sum(log_keep[:, cs], axis=1, keepdims=True)
        att = jnp.concatenate(pieces, axis=1)
        o = jnp.dot(att, v_scr[h, 0:width, :], preferred_element_type=F32)
        o_scr[h, pl.ds(row0, CHUNK), :] = o.astype(o_scr.dtype)

    _for_heads_and_tiles(k_scr.shape[1] // CHUNK, tile)
    _unstage_heads(o_ref, o_scr, t)


def sb_prompt(proj16):
    b, t, _ = proj16.shape
    tp = pl.cdiv(t, CHUNK) * CHUNK
    blk = lambda off: pl.BlockSpec((1, t, MIX_W), lambda i, o=off // MIX_W: (i, 0, o))
    return pl.pallas_call(
        functools.partial(_sb_prompt_kernel, t=t),
        grid=(b,),
        in_specs=[blk(OFF_QB), blk(OFF_KB), blk(OFF_VB)],
        out_specs=pl.BlockSpec((1, t, MIX_W), lambda i: (i, 0, 0)),
        out_shape=jax.ShapeDtypeStruct((b, t, MIX_W), BF16),
        scratch_shapes=_head_scratch(tp, 4),
        compiler_params=_cparams("parallel"),
        name="sb_prompt",
    )(proj16, proj16, proj16)


def _order_key(score):
    score = jnp.where(score == 0.0, 0.0, score)
    bits = lax.bitcast_convert_type(score, I32)
    return bits ^ ((bits >> 31) & 0x7FFFFFFF)


def _select_topk(key_ref, bias_ref, vis_fn, rows, width, topk):
    kf = float(topk)

    def count(pred):
        return jnp.sum(jnp.where(pred(key_ref[0:rows, 0:width]), 1.0, 0.0), axis=1, keepdims=True)

    thr0 = jnp.where(count(lambda k: k >= 0) >= kf, 0, INT_MIN).astype(I32)

    def bit_step(b, thr):
        cand = thr | (jnp.int32(1) << (30 - b))
        return jnp.where(count(lambda k: k >= cand) >= kf, cand, thr)

    thr = lax.fori_loop(0, 31, bit_step, thr0)
    need = kf - count(lambda k: k > thr)
    keys = key_ref[0:rows, 0:width]
    vis = vis_fn(0, width)
    n_tie = jnp.sum(jnp.where((keys == thr) & vis, 1.0, 0.0), axis=1, keepdims=True)
    bias_ref[0:rows, 0:width] = jnp.where((keys >= thr) & vis, 0.0, NEG_INF)

    @pl.when(jnp.max(n_tie - need) > 0.0)
    def _():
        prefix = (_iota((CHUNK, CHUNK), 0) <= _iota((CHUNK, CHUNK), 1)).astype(BF16)
        seen = jnp.zeros((rows, 1), F32)
        for c in range(width // CHUNK):
            k = key_ref[0:rows, c * CHUNK:(c + 1) * CHUNK]
            v = vis_fn(c * CHUNK, CHUNK)
            tie = jnp.where((k == thr) & v, 1.0, 0.0)
            rank = jnp.dot(tie.astype(BF16), prefix, preferred_element_type=F32) + seen
            sel = ((k > thr) & v) | ((tie > 0.0) & (rank <= need))
            bias_ref[0:rows, c * CHUNK:(c + 1) * CHUNK] = jnp.where(sel, 0.0, NEG_INF)
            seen = seen + jnp.sum(tie, axis=1, keepdims=True)


def _stack_heads(q):
    return jnp.concatenate([q[:, h * HEAD_DIM:(h + 1) * HEAD_DIM] for h in range(N_HEADS)], axis=0)


def _dsa_prompt_kernel(q_ref, qi_ref, kc_ref, vc_ref, misc32_ref, misc16_ref, o_ref,
                       kc_scr, vc_scr, ki_scr, key_scr, bias_scr, *, t, topk):
    n_full, tail = t // CHUNK, t % CHUNK
    scale = HEAD_DIM ** -0.5
    idx_scale = (IDX_DIM * IDX_HEADS) ** -0.5
    _pad_copy(kc_scr, kc_ref[0], t)
    _pad_copy(vc_scr, vc_ref[0], t)
    _pad_copy(ki_scr, misc16_ref[0], t)

    def tile(row0, tq, width):
        def vis_fn(c0, w):
            return (c0 + _iota((tq, w), 1)) <= (row0 + _iota((tq, w), 0))

        qi = qi_ref[0, pl.ds(row0, tq), :]
        wi = misc32_ref[0, pl.ds(row0, tq), MISC_WI:MISC_WI + IDX_HEADS]
        ki = ki_scr[0:width, MISC_KI:MISC_KI + IDX_DIM]
        score = None
        for h in range(IDX_HEADS):
            term = wi[:, h:h + 1] * jnp.maximum(_dot_nt(qi[:, h * IDX_DIM:(h + 1) * IDX_DIM], ki), 0.0)
            score = term if score is None else score + term
        key_scr[0:tq, 0:width] = jnp.where(vis_fn(0, width), _order_key(score * idx_scale), INT_MIN)
        _select_topk(key_scr, bias_scr, vis_fn, tq, width, topk)

        q4 = _stack_heads(q_ref[0, pl.ds(row0, tq), :])
        bias = bias_scr[0:tq, 0:width]
        s = _dot_nt(q4, kc_scr[0:width, :]) * scale + jnp.concatenate([bias] * N_HEADS, axis=0)
        p = jnp.exp(s - jnp.max(s, axis=1, keepdims=True))
        l = jnp.sum(p, axis=1, keepdims=True)
        out = jnp.dot(p.astype(BF16), vc_scr[0:width, :], preferred_element_type=F32) / l
        for h in range(N_HEADS):
            o_ref[0, pl.ds(row0, tq), h * HEAD_DIM:(h + 1) * HEAD_DIM] = out[h * tq:(h + 1) * tq].astype(o_ref.dtype)

    for lo, hi in _bands(n_full):
        def body(i, _, hi=hi):
            tile(pl.multiple_of(i * CHUNK, CHUNK), CHUNK, hi * CHUNK)
            return 0
        lax.fori_loop(lo, hi, body, 0)
    if tail:
        tile(n_full * CHUNK, tail, (n_full + 1) * CHUNK)


def dsa_prompt(proj32, proj16, topk):
    b, t, _ = proj16.shape
    tp = pl.cdiv(t, CHUNK) * CHUNK
    wide = lambda off: pl.BlockSpec((1, t, MIX_W), lambda i, o=off // MIX_W: (i, 0, o))
    narrow = lambda off: pl.BlockSpec((1, t, LANES), lambda i, o=off // LANES: (i, 0, o))
    return pl.pallas_call(
        functools.partial(_dsa_prompt_kernel, t=t, topk=topk),
        grid=(b,),
        in_specs=[wide(OFF_QC), wide(OFF_QI), narrow(OFF_KC), narrow(OFF_VC), narrow(OFF_MISC), narrow(OFF_MISC)],
        out_specs=pl.BlockSpec((1, t, MIX_W), lambda i: (i, 0, 0)),
        out_shape=jax.ShapeDtypeStruct((b, t, MIX_W), BF16),
        scratch_shapes=[pltpu.VMEM((tp, HEAD_DIM), BF16), pltpu.VMEM((tp, HEAD_DIM), BF16),
                        pltpu.VMEM((tp, LANES), BF16),
                        pltpu.VMEM((CHUNK, tp), I32), pltpu.VMEM((CHUNK, tp), F32)],
        compiler_params=_cparams("parallel"),
        name="dsa_prompt",
    )(proj16, proj16, proj16, proj16, proj32, proj16)


CONV_PAD = 8


def _rglru_kernel(xd_ref, cs_ref, h0_ref, cw_ref, cb_ref, wa_ref, ba_ref, wx_ref, bx_ref, lam_ref,
                  y_ref, hl_ref, xpad_scr, a_scr, u_scr, hs_scr, *, t):
    n_full, tail = t // CHUNK, t % CHUNK
    xpad_scr[pl.ds(0, CONV_PAD), :] = jnp.zeros((CONV_PAD, MIX_W), F32)
    xpad_scr[pl.ds(CONV_PAD - (CONV_W - 1), CONV_W - 1), :] = cs_ref[0]
    xpad_scr[pl.ds(CONV_PAD, t), :] = xd_ref[0]
    decay_rate = -RG_C * _softplus(-lam_ref[...])

    def chunk(r0, rows, h):
        win = xpad_scr[pl.ds(r0, rows + CONV_PAD), :]
        xc = cb_ref[...]
        for i in range(CONV_W):
            lo = CONV_PAD - (CONV_W - 1) + i
            xc = xc + win[lo:lo + rows] * cw_ref[i:i + 1, :]
        xc16 = xc.astype(BF16)
        gate_r = jax.nn.sigmoid(jnp.dot(xc16, wa_ref[...], preferred_element_type=F32) + ba_ref[...])
        gate_i = jax.nn.sigmoid(jnp.dot(xc16, wx_ref[...], preferred_element_type=F32) + bx_ref[...])
        log_a = gate_r * decay_rate
        a_scr[0:rows, :] = jnp.exp(log_a)
        th = jnp.tanh(log_a)
        u_scr[0:rows, :] = jnp.sqrt(-2.0 * th / (1.0 - th)) * gate_i * xc

        def step(r, hh):
            hh = a_scr[pl.ds(r, 1), :] * hh + u_scr[pl.ds(r, 1), :]
            hs_scr[pl.ds(r, 1), :] = hh
            return hh
        h = lax.fori_loop(0, rows, step, h, unroll=8)
        y_ref[0, pl.ds(r0, rows), :] = hs_scr[0:rows, :].astype(y_ref.dtype)
        return h

    h = h0_ref[0]
    if n_full:
        h = lax.fori_loop(0, n_full, lambda i, hh: chunk(pl.multiple_of(i * CHUNK, CHUNK), CHUNK, hh), h)
    if tail:
        h = chunk(n_full * CHUNK, tail, h)
    hl_ref[0] = h


def rglru(proj32, conv_state, h0, lw, out_dtype):
    b, t, _ = proj32.shape
    conv_w, conv_b, wa16, b_a, wx16, b_x, lam = (lw[k] for k in ("conv_w", "conv_b", "w_a", "b_a", "w_x", "b_x", "lam"))
    row = lambda v: v.reshape(1, MIX_W)
    const = lambda shape: pl.BlockSpec(shape, lambda i: (0,) * len(shape))
    return pl.pallas_call(
        functools.partial(_rglru_kernel, t=t),
        grid=(b,),
        in_specs=[pl.BlockSpec((1, t, MIX_W), lambda i: (i, 0, OFF_XD // MIX_W)),
                  pl.BlockSpec((1, CONV_W - 1, MIX_W), lambda i: (i, 0, 0)),
                  pl.BlockSpec((1, 1, MIX_W), lambda i: (i, 0, 0)),
                  const((CONV_W, MIX_W)), const((1, MIX_W)),
                  const((MIX_W, MIX_W)), const((1, MIX_W)),
                  const((MIX_W, MIX_W)), const((1, MIX_W)), const((1, MIX_W))],
        out_specs=[pl.BlockSpec((1, t, MIX_W), lambda i: (i, 0, 0)),
                   pl.BlockSpec((1, 1, MIX_W), lambda i: (i, 0, 0))],
        out_shape=[jax.ShapeDtypeStruct((b, t, MIX_W), out_dtype), jax.ShapeDtypeStruct((b, 1, MIX_W), F32)],
        scratch_shapes=[pltpu.VMEM((CONV_PAD + pl.cdiv(t, CHUNK) * CHUNK + CONV_PAD, MIX_W), F32),
                        pltpu.VMEM((CHUNK, MIX_W), F32), pltpu.VMEM((CHUNK, MIX_W), F32),
                        pltpu.VMEM((CHUNK, MIX_W), F32)],
        compiler_params=_cparams("parallel"),
        name="rglru",
    )(proj32, conv_state, h0.reshape(b, 1, MIX_W), conv_w, row(conv_b), wa16, row(b_a), wx16, row(b_x), row(lam))


def _block_diag(w):
    n, c, e = w.shape
    eye = jnp.eye(n, dtype=w.dtype)
    return (eye[:, None, :, None] * w[:, :, None, :]).reshape(n * c, n * e)


def _merge_kernel(h_ref, ya_ref, yb_ref, yc_ref, yd_ref, g0_ref, g1_ref, g2_ref, g3_ref, wb_ref, o_ref):
    h = h_ref[...]
    acc = None
    for n, (y_ref, g_ref) in enumerate(((ya_ref, g0_ref), (yb_ref, g1_ref), (yc_ref, g2_ref), (yd_ref, g3_ref))):
        gate = jax.nn.sigmoid(jnp.dot(h, g_ref[...], preferred_element_type=F32))
        term = gate * jnp.dot(y_ref[...], wb_ref[n], preferred_element_type=F32)
        acc = term if acc is None else acc + term
    o_ref[...] = acc.astype(o_ref.dtype)


def merge(h16, ys, wg16, wb16, tm_pref=512, tn=512):
    m, d = h16.shape
    tm = _row_tile(m, tm_pref)
    nj = d // tn
    y_spec = pl.BlockSpec((tm, MIX_W), lambda i, j: (i, 0))
    g_spec = lambda n: pl.BlockSpec((d, tn), lambda i, j, n=n: (0, n * nj + j))
    return pl.pallas_call(
        _merge_kernel,
        grid=(m // tm, nj),
        in_specs=[pl.BlockSpec((tm, d), lambda i, j: (i, 0)), y_spec, y_spec, y_spec, y_spec,
                  g_spec(0), g_spec(1), g_spec(2), g_spec(3),
                  pl.BlockSpec((4, MIX_W, tn), lambda i, j: (0, 0, j))],
        out_specs=pl.BlockSpec((tm, tn), lambda i, j: (i, j)),
        out_shape=jax.ShapeDtypeStruct((m, d), BF16),
        compiler_params=_cparams("parallel", "arbitrary"),
        name="merge",
    )(h16, *ys, wg16, wg16, wg16, wg16, wb16)


def _matmul_res_kernel(a_ref, w_ref, x_ref, o_ref):
    o_ref[...] = x_ref[...] + jnp.dot(a_ref[...], w_ref[...], preferred_element_type=F32)


def matmul_residual(a16, w16, x, tm_pref=344):
    m, k = a16.shape
    n = w16.shape[1]
    tm = _row_tile(m, tm_pref)
    return pl.pallas_call(
        _matmul_res_kernel,
        grid=(m // tm,),
        in_specs=[pl.BlockSpec((tm, k), lambda i: (i, 0)),
                  pl.BlockSpec((k, n), lambda i: (0, 0)),
                  pl.BlockSpec((tm, n), lambda i: (i, 0))],
        out_specs=pl.BlockSpec((tm, n), lambda i: (i, 0)),
        out_shape=jax.ShapeDtypeStruct((m, n), F32),
        compiler_params=_cparams("parallel"),
        name="out_proj",
    )(a16, w16, x)


def _ffn_kernel(x_ref, g_ref, wg_ref, wu_ref, wd_ref, o_ref, h_scr, acc_scr):
    f = pl.program_id(1)

    @pl.when(f == 0)
    def _():
        x = x_ref[...]
        y = x * lax.rsqrt(jnp.mean(x * x, axis=-1, keepdims=True) + RMS_EPS)
        h_scr[...] = (y * g_ref[...]).astype(BF16)
        acc_scr[...] = jnp.zeros_like(acc_scr)

    h = h_scr[...]
    gate = jnp.dot(h, wg_ref[...], preferred_element_type=F32)
    up = jnp.dot(h, wu_ref[...], preferred_element_type=F32)
    act = (jax.nn.silu(gate) * up).astype(BF16)
    acc_scr[...] += jnp.dot(act, wd_ref[...], preferred_element_type=F32)

    @pl.when(f == pl.num_programs(1) - 1)
    def _():
        o_ref[...] = x_ref[...] + acc_scr[...]


def ffn(x, g, wg16, wu16, wd16, tm_pref=688, tf=512):
    m, d = x.shape
    dff = wg16.shape[1]
    tm = _row_tile(m, tm_pref)
    return pl.pallas_call(
        _ffn_kernel,
        grid=(m // tm, dff // tf),
        in_specs=[pl.BlockSpec((tm, d), lambda i, f: (i, 0)),
                  pl.BlockSpec((1, d), lambda i, f: (0, 0)),
                  pl.BlockSpec((d, tf), lambda i, f: (0, f)),
                  pl.BlockSpec((d, tf), lambda i, f: (0, f)),
                  pl.BlockSpec((tf, d), lambda i, f: (f, 0))],
        out_specs=pl.BlockSpec((tm, d), lambda i, f: (i, 0)),
        out_shape=jax.ShapeDtypeStruct((m, d), F32),
        scratch_shapes=[pltpu.VMEM((tm, d), BF16), pltpu.VMEM((tm, d), F32)],
        compiler_params=_cparams("parallel", "arbitrary"),
        name="ffn",
    )(x, g.reshape(1, d), wg16, wu16, wd16)


def _rmsnorm_kernel(x_ref, g_ref, o_ref):
    x = x_ref[...]
    o_ref[...] = x * lax.rsqrt(jnp.mean(x * x, axis=-1, keepdims=True) + RMS_EPS) * g_ref[...]


def rmsnorm(x, g, tm_pref=512):
    m, d = x.shape
    tm = _row_tile(m, tm_pref)
    return pl.pallas_call(
        _rmsnorm_kernel,
        grid=(m // tm,),
        in_specs=[pl.BlockSpec((tm, d), lambda i: (i, 0)), pl.BlockSpec((1, d), lambda i: (0, 0))],
        out_specs=pl.BlockSpec((tm, d), lambda i: (i, 0)),
        out_shape=jax.ShapeDtypeStruct((m, d), F32),
        compiler_params=_cparams("parallel"),
        name="final_norm",
    )(x, g.reshape(1, d))


PAGES_PER_STEP = 8
STEP_KEYS = PAGES_PER_STEP * PAGE


def _page_specs(block, layer, n_pages, reverse):
    def spec(p):
        def index(b, s, pt):
            j = s * PAGES_PER_STEP + p
            if reverse:
                j = n_pages - 1 - j
            return (layer, pt[b, j], 0, 0)
        return pl.BlockSpec((1, 1) + block, index)
    return [spec(p) for p in range(PAGES_PER_STEP)]


def _per_seq(shape):
    return pl.BlockSpec((1,) + shape, lambda b, s, pt: (b,) + (0,) * len(shape))


def _head_page(ref, h):
    return ref.at[0, 0][pl.ds(h, PAGE, stride=N_HEADS), :].astype(BF16)


def _own_head_rows(per_head, t):
    return jnp.concatenate([per_head[h][h * t:(h + 1) * t] for h in range(N_HEADS)], axis=0)


def _query_index(t, width):
    return jnp.concatenate([_iota((t, width), 0)] * N_HEADS, axis=0)


def _pad_rows(dst, src):
    dst[...] = jnp.zeros(dst.shape, dst.dtype)
    dst[pl.ds(0, src.shape[0]), :] = src


def _per_head_rows(x, t):
    return jnp.concatenate([jnp.broadcast_to(x[h:h + 1], (t, x.shape[1])) for h in range(N_HEADS)], axis=0)


def _per_head_cols(x):
    return jnp.concatenate([x[:, h:h + 1] for h in range(N_HEADS)], axis=0)


def _write_stacked(o_ref, out, t):
    for h in range(N_HEADS):
        o_ref[0, :, h * HEAD_DIM:(h + 1) * HEAD_DIM] = out[h * t:(h + 1) * t].astype(o_ref.dtype)


def _softmax_update(carry, s, pv_fn):
    m, l, acc = carry
    m_new = jnp.maximum(m, jnp.max(s, axis=1, keepdims=True))
    m_safe = jnp.where(m_new == NEG_INF, 0.0, m_new)
    alpha = jnp.exp(m - m_safe)
    p = jnp.exp(s - m_safe)
    l = alpha * l + jnp.sum(p, axis=1, keepdims=True)
    return m_new, l, alpha * acc + pv_fn(p.astype(BF16))


def _softmax_init(rows):
    return (jnp.full((rows, 1), NEG_INF, F32), jnp.zeros((rows, 1), F32), jnp.zeros((rows, HEAD_DIM), F32))


def _new_keys_per_head(q4, pad_ref, t):
    return _own_head_rows([_dot_nt(q4, pad_ref[:, h * HEAD_DIM:(h + 1) * HEAD_DIM].astype(BF16))
                           for h in range(N_HEADS)], t)


def _new_values_per_head(p16, pad_ref, t):
    return _own_head_rows([jnp.dot(p16, pad_ref[:, h * HEAD_DIM:(h + 1) * HEAD_DIM].astype(BF16),
                                   preferred_element_type=F32) for h in range(N_HEADS)], t)


def _page_scores(q4, k_refs, t):
    return [_own_head_rows([_dot_nt(q4, _head_page(k_ref, h)) for h in range(N_HEADS)], t) for k_ref in k_refs]


def _page_values(p16, v_refs, t):
    acc = None
    for n, v_ref in enumerate(v_refs):
        pp = p16[:, n * PAGE:(n + 1) * PAGE]
        term = _own_head_rows([jnp.dot(pp, _head_page(v_ref, h), preferred_element_type=F32)
                               for h in range(N_HEADS)], t)
        acc = term if acc is None else acc + term
    return acc


def _fox_sample_kernel(pt_ref, q_ref, kn_ref, vn_ref, cn_ref, cnrow_ref, *rest, t):
    n = PAGES_PER_STEP
    k_refs, v_refs, lf_refs = rest[0:n], rest[n:2 * n], rest[2 * n:3 * n]
    o_ref, kpad, vpad, m_scr, l_scr, acc_scr, d_scr = rest[3 * n:]
    s_id = pl.program_id(1)
    scale = HEAD_DIM ** -0.5
    rows = N_HEADS * t
    q4 = _stack_heads(q_ref[0]).astype(BF16)
    cn = _per_head_cols(cn_ref[0])

    @pl.when(s_id == 0)
    def _():
        _pad_rows(kpad, kn_ref[0])
        _pad_rows(vpad, vn_ref[0])
        s = _new_keys_per_head(q4, kpad, t) * scale + cn - _per_head_rows(cnrow_ref[0], t)
        col = _iota((rows, CHUNK), 1)
        s = jnp.where((col <= _query_index(t, CHUNK)) & (col < t), s, NEG_INF)
        m, l, acc = _softmax_update(_softmax_init(rows), s, lambda p16: _new_values_per_head(p16, vpad, t))
        m_scr[...], l_scr[...], acc_scr[...] = m, l, acc
        d_scr[...] = jnp.zeros(d_scr.shape, F32)

    suffix = _suffix_matrix()
    later = d_scr[...]
    scores = _page_scores(q4, k_refs, t)
    for p in range(n):
        lf = jnp.concatenate([lf_refs[p][0, 0], jnp.zeros((SUBLANES - N_HEADS, PAGE), F32)], axis=0)
        decay = _dot01_right(lf, suffix) + later
        later = later + jnp.sum(lf, axis=1, keepdims=True)
        scores[p] = scores[p] * scale + _per_head_rows(decay[0:N_HEADS], t)
    s = jnp.concatenate(scores, axis=1) + cn
    carry = _softmax_update((m_scr[...], l_scr[...], acc_scr[...]), s, lambda p16: _page_values(p16, v_refs, t))
    m_scr[...], l_scr[...], acc_scr[...] = carry
    d_scr[...] = later

    @pl.when(s_id == pl.num_programs(1) - 1)
    def _():
        _write_stacked(o_ref, carry[2] / carry[1], t)


def _kv_page_block():
    return (PAGE * N_HEADS, HEAD_DIM)


def fox_sample(page_table, layer, q, kn, vn, cn, cnrow, cache_k, cache_v, cache_lft):
    b, t, _ = q.shape
    n_pages = page_table.shape[1]
    rows = N_HEADS * t
    grid_spec = pltpu.PrefetchScalarGridSpec(
        num_scalar_prefetch=1,
        grid=(b, n_pages // PAGES_PER_STEP),
        in_specs=[_per_seq((t, MIX_W)), _per_seq((t, MIX_W)), _per_seq((t, MIX_W)),
                  _per_seq((t, N_HEADS)), _per_seq((N_HEADS, CHUNK))]
                 + _page_specs(_kv_page_block(), layer, n_pages, True)
                 + _page_specs(_kv_page_block(), layer, n_pages, True)
                 + _page_specs((N_HEADS, PAGE), layer, n_pages, True),
        out_specs=_per_seq((t, MIX_W)),
        scratch_shapes=[pltpu.VMEM((CHUNK, MIX_W), F32), pltpu.VMEM((CHUNK, MIX_W), F32),
                        pltpu.VMEM((rows, 1), F32), pltpu.VMEM((rows, 1), F32), pltpu.VMEM((rows, HEAD_DIM), F32),
                        pltpu.VMEM((SUBLANES, 1), F32)])
    return pl.pallas_call(
        functools.partial(_fox_sample_kernel, t=t),
        grid_spec=grid_spec,
        out_shape=jax.ShapeDtypeStruct((b, t, MIX_W), F32),
        compiler_params=_cparams("parallel", "arbitrary"),
        name="fox_sample",
    )(page_table, q, kn, vn, cn, cnrow, *([cache_k] * PAGES_PER_STEP), *([cache_v] * PAGES_PER_STEP),
      *([cache_lft] * PAGES_PER_STEP))


def _sb_sample_kernel(pt_ref, q_ref, kn_ref, vn_ref, *rest, t):
    n = PAGES_PER_STEP
    k_refs, v_refs = rest[0:n], rest[n:2 * n]
    o_ref, kpad, vpad, r_scr, acc_scr = rest[2 * n:]
    s_id = pl.program_id(1)
    scale = HEAD_DIM ** -0.5
    rows = N_HEADS * t
    q4 = _stack_heads(q_ref[0]).astype(BF16)
    suffix = _suffix_matrix()

    @pl.when(s_id == 0)
    def _():
        _pad_rows(kpad, kn_ref[0])
        _pad_rows(vpad, vn_ref[0])
        z = _new_keys_per_head(q4, kpad, t) * scale
        vis = _iota((rows, CHUNK), 1) < _query_index(t, CHUNK)
        log_keep = jnp.where(vis, -_softplus(z), 0.0)
        att = jnp.where(vis, jnp.exp(log_keep + z + _dot01_right(log_keep, suffix)), 0.0)
        acc_scr[...] = _new_values_per_head(att.astype(BF16), vpad, t)
        r_scr[...] = jnp.sum(log_keep, axis=1, keepdims=True)

    later = r_scr[...]
    pieces = []
    for z in _page_scores(q4, k_refs, t):
        z = z * scale
        log_keep = -_softplus(z)
        pieces.append(jnp.exp(log_keep + z + _dot01_right(log_keep, suffix) + later).astype(BF16))
        later = later + jnp.sum(log_keep, axis=1, keepdims=True)
    acc = acc_scr[...] + _page_values(jnp.concatenate(pieces, axis=1), v_refs, t)
    r_scr[...], acc_scr[...] = later, acc

    @pl.when(s_id == pl.num_programs(1) - 1)
    def _():
        _write_stacked(o_ref, acc, t)


def sb_sample(page_table, layer, q, kn, vn, cache_k, cache_v):
    b, t, _ = q.shape
    n_pages = page_table.shape[1]
    rows = N_HEADS * t
    grid_spec = pltpu.PrefetchScalarGridSpec(
        num_scalar_prefetch=1,
        grid=(b, n_pages // PAGES_PER_STEP),
        in_specs=[_per_seq((t, MIX_W))] * 3
                 + _page_specs(_kv_page_block(), layer, n_pages, True)
                 + _page_specs(_kv_page_block(), layer, n_pages, True),
        out_specs=_per_seq((t, MIX_W)),
        scratch_shapes=[pltpu.VMEM((CHUNK, MIX_W), F32), pltpu.VMEM((CHUNK, MIX_W), F32),
                        pltpu.VMEM((rows, 1), F32), pltpu.VMEM((rows, HEAD_DIM), F32)])
    return pl.pallas_call(
        functools.partial(_sb_sample_kernel, t=t),
        grid_spec=grid_spec,
        out_shape=jax.ShapeDtypeStruct((b, t, MIX_W), F32),
        compiler_params=_cparams("parallel", "arbitrary"),
        name="sb_sample",
    )(page_table, q, kn, vn, *([cache_k] * PAGES_PER_STEP), *([cache_v] * PAGES_PER_STEP))


def _weighted_relu_sum(g, wi, t):
    score = None
    for h in range(IDX_HEADS):
        term = wi[:, h:h + 1] * jnp.maximum(g[h * t:(h + 1) * t], 0.0)
        score = term if score is None else score + term
    return score * ((IDX_DIM * IDX_HEADS) ** -0.5)


def _dsa_scores_kernel(pt_ref, qi_ref, misc_ref, *rest, t):
    n = PAGES_PER_STEP
    idx_refs = rest[0:n]
    past_ref, new_ref, kpad = rest[n:]
    qi = qi_ref[0]
    qi_stack = jnp.concatenate([qi[:, h * IDX_DIM:(h + 1) * IDX_DIM] for h in range(IDX_HEADS)], axis=0).astype(BF16)
    wi = misc_ref[0, :, MISC_WI:MISC_WI + IDX_HEADS]

    @pl.when(pl.program_id(1) == 0)
    def _():
        _pad_rows(kpad, misc_ref[0])
        g = _dot_nt(qi_stack, kpad[:, MISC_KI:MISC_KI + IDX_DIM].astype(BF16))
        new_ref[0] = _weighted_relu_sum(g, wi, t)

    for p in range(n):
        g = jnp.dot(qi_stack, idx_refs[p][0, 0].astype(BF16), preferred_element_type=F32)
        past_ref[0, :, p * PAGE:(p + 1) * PAGE] = _weighted_relu_sum(g, wi, t)


def dsa_sample_scores(page_table, layer, qi, misc, cache_idx_t):
    b, t, _ = qi.shape
    n_pages = page_table.shape[1]
    grid_spec = pltpu.PrefetchScalarGridSpec(
        num_scalar_prefetch=1,
        grid=(b, n_pages // PAGES_PER_STEP),
        in_specs=[_per_seq((t, MIX_W)), _per_seq((t, LANES))] + _page_specs((IDX_DIM, PAGE), layer, n_pages, False),
        out_specs=[pl.BlockSpec((1, t, STEP_KEYS), lambda i, s, pt: (i, 0, s)), _per_seq((t, CHUNK))],
        scratch_shapes=[pltpu.VMEM((CHUNK, LANES), F32)])
    return pl.pallas_call(
        functools.partial(_dsa_scores_kernel, t=t),
        grid_spec=grid_spec,
        out_shape=[jax.ShapeDtypeStruct((b, t, n_pages * PAGE), F32), jax.ShapeDtypeStruct((b, t, CHUNK), F32)],
        compiler_params=_cparams("parallel", "arbitrary"),
        name="dsa_sample_scores",
    )(page_table, qi, misc, *([cache_idx_t] * PAGES_PER_STEP))


SELECT_ROWS = 64


def _dsa_select_kernel(score_ref, bias_ref, key_scr, *, t, n_past, topk):
    rows, width = score_ref.shape
    qidx = jnp.concatenate([_iota((t, 1), 0)] * (rows // t), axis=0)

    def vis_fn(c0, w):
        col = c0 + _iota((rows, w), 1)
        return (col < n_past) | ((col - n_past <= qidx) & (col - n_past < t))

    key_scr[...] = jnp.where(vis_fn(0, width), _order_key(score_ref[...]), INT_MIN)
    _select_topk(key_scr, bias_ref, vis_fn, rows, width, topk)


def dsa_sample_select(scores, t, n_past, topk):
    m, width = scores.shape
    rows = _row_tile(m, SELECT_ROWS)
    assert rows % t == 0
    spec = pl.BlockSpec((rows, width), lambda i: (i, 0))
    return pl.pallas_call(
        functools.partial(_dsa_select_kernel, t=t, n_past=n_past, topk=topk),
        grid=(m // rows,),
        in_specs=[spec],
        out_specs=spec,
        out_shape=jax.ShapeDtypeStruct(scores.shape, F32),
        scratch_shapes=[pltpu.VMEM((rows, width), I32)],
        compiler_params=_cparams("parallel"),
        name="dsa_sample_select",
    )(scores)


def _dsa_attend_kernel(pt_ref, q_ref, kn_ref, vn_ref, bias_new_ref, bias_past_ref, *rest, t):
    n = PAGES_PER_STEP
    k_refs, v_refs = rest[0:n], rest[n:2 * n]
    o_ref, kpad, vpad, m_scr, l_scr, acc_scr = rest[2 * n:]
    s_id = pl.program_id(1)
    scale = HEAD_DIM ** -0.5
    rows = N_HEADS * t
    q4 = _stack_heads(q_ref[0]).astype(BF16)

    @pl.when(s_id == 0)
    def _():
        _pad_rows(kpad, kn_ref[0])
        _pad_rows(vpad, vn_ref[0])
        s = _dot_nt(q4, kpad[...].astype(BF16)) * scale + jnp.concatenate([bias_new_ref[0]] * N_HEADS, axis=0)
        m, l, acc = _softmax_update(
            _softmax_init(rows), s,
            lambda p16: jnp.dot(p16, vpad[...].astype(BF16), preferred_element_type=F32))
        m_scr[...], l_scr[...], acc_scr[...] = m, l, acc

    s = jnp.concatenate([_dot_nt(q4, k_ref[0, 0].astype(BF16)) for k_ref in k_refs], axis=1) * scale
    s = s + jnp.concatenate([bias_past_ref[0]] * N_HEADS, axis=0)

    def pv(p16):
        acc = None
        for p, v_ref in enumerate(v_refs):
            term = jnp.dot(p16[:, p * PAGE:(p + 1) * PAGE], v_ref[0, 0].astype(BF16), preferred_element_type=F32)
            acc = term if acc is None else acc + term
        return acc
    carry = _softmax_update((m_scr[...], l_scr[...], acc_scr[...]), s, pv)
    m_scr[...], l_scr[...], acc_scr[...] = carry

    @pl.when(s_id == pl.num_programs(1) - 1)
    def _():
        _write_stacked(o_ref, carry[2] / carry[1], t)


def dsa_sample_attend(page_table, layer, q, kn, vn, bias, cache_k, cache_v):
    b, t, _ = q.shape
    n_pages = page_table.shape[1]
    rows = N_HEADS * t
    grid_spec = pltpu.PrefetchScalarGridSpec(
        num_scalar_prefetch=1,
        grid=(b, n_pages // PAGES_PER_STEP),
        in_specs=[_per_seq((t, MIX_W)), _per_seq((t, HEAD_DIM)), _per_seq((t, HEAD_DIM)),
                  pl.BlockSpec((1, t, CHUNK), lambda i, s, pt: (i, 0, n_pages)),
                  pl.BlockSpec((1, t, STEP_KEYS), lambda i, s, pt: (i, 0, s))]
                 + _page_specs((PAGE, HEAD_DIM), layer, n_pages, False)
                 + _page_specs((PAGE, HEAD_DIM), layer, n_pages, False),
        out_specs=_per_seq((t, MIX_W)),
        scratch_shapes=[pltpu.VMEM((CHUNK, HEAD_DIM), F32), pltpu.VMEM((CHUNK, HEAD_DIM), F32),
                        pltpu.VMEM((rows, 1), F32), pltpu.VMEM((rows, 1), F32), pltpu.VMEM((rows, HEAD_DIM), F32)])
    return pl.pallas_call(
        functools.partial(_dsa_attend_kernel, t=t),
        grid_spec=grid_spec,
        out_shape=jax.ShapeDtypeStruct((b, t, MIX_W), F32),
        compiler_params=_cparams("parallel", "arbitrary"),
        name="dsa_sample_attend",
    )(page_table, q, kn, vn, bias, bias, *([cache_k] * PAGES_PER_STEP), *([cache_v] * PAGES_PER_STEP))


def _pad_w_in(w):
    widths = (512, 512, 512, 4, 512, 512, 512, 512, 128, 128, 512, 64, 8, 512)
    names = ("qa", "ka", "va", "fa", "qb", "kb", "vb", "qc", "kc", "vc", "qi", "ki", "wi", "xd")
    seg, off = {}, 0
    for name, width in zip(names, widths):
        seg[name] = w[:, off:off + width]
        off += width
    order = ("qa", "ka", "va", "qb", "kb", "vb", "qc", "qi", "xd", "kc", "vc", "ki", "fa", "wi")
    cols = jnp.concatenate([seg[n] for n in order], axis=1)
    return jnp.pad(cols, ((0, 0), (0, N_PROJ - cols.shape[1]))).astype(BF16)


def _cols(p, off, width):
    return p[..., off:off + width]


def _new_rows(p32, lf, b, t):
    heads = lambda off: _cols(p32, off, MIX_W).reshape(b, t, N_HEADS, HEAD_DIM)
    return (heads(OFF_KA), heads(OFF_VA), _cols(lf, MISC_FA, N_HEADS), heads(OFF_KB), heads(OFF_VB),
            _cols(p32, OFF_KC, HEAD_DIM), _cols(p32, OFF_VC, HEAD_DIM), _cols(p32, OFF_MISC + MISC_KI, IDX_DIM))


def _dense_tail(x2d, h16, ys, lw):
    merged = merge(h16, ys, lw["w_gate"], lw["w_branch"])
    x2d = matmul_residual(merged, lw["w_out"], x2d)
    return ffn(x2d, lw["g_ffn"], lw["w_fg"], lw["w_fu"], lw["w_fd"])


def _forget_bias_row(b_f):
    return jnp.zeros((1, LANES), F32).at[0, MISC_FA:MISC_FA + N_HEADS].set(b_f)


def _conv_tail(conv_state, p32):
    xd = _cols(p32, OFF_XD, MIX_W)
    return jnp.concatenate([conv_state, xd], axis=1)[:, -(CONV_W - 1):]


def _prompt_layer(x2d, b, t, lw, topk):
    p32, p16, h16 = norm_proj(x2d, lw["g_mix"], lw["w_in"])
    p32, p16 = p32.reshape(b, t, N_PROJ), p16.reshape(b, t, N_PROJ)
    lf, fc = logf_cumsum(_cols(p32, OFF_MISC, LANES), _forget_bias_row(lw["b_f"]))
    tp = pl.cdiv(t, CHUNK) * CHUNK
    frow = jnp.pad(_cols(fc, MISC_FA, N_HEADS), ((0, 0), (0, tp - t), (0, 0))).transpose(0, 2, 1)
    conv0 = jnp.zeros((b, CONV_W - 1, MIX_W), F32)
    ya = fox_prompt(p16, frow.reshape(b, N_HEADS, 1, tp))
    yb = sb_prompt(p16)
    yc = dsa_prompt(p32, p16, topk)
    yd, h_last = rglru(p32, conv0, jnp.zeros((b, MIX_W), F32), lw, BF16)
    ys = [y.reshape(b * t, MIX_W) for y in (ya, yb, yc, yd)]
    rows = _new_rows(p32, lf, b, t) + (_conv_tail(conv0, p32), h_last.reshape(b, MIX_W))
    return _dense_tail(x2d, h16, ys, lw), rows


def _sample_layer(x2d, b, t, lw, topk, layer, page_table, caches, conv_state, h0):
    ca_k, ca_v, ca_lft, cb_k, cb_v, cc_k, cc_v, cc_idx_t = caches
    n_past = page_table.shape[1] * PAGE
    p32, _, h16 = norm_proj(x2d, lw["g_mix"], lw["w_in"])
    p32 = p32.reshape(b, t, N_PROJ)
    misc = _cols(p32, OFF_MISC, LANES)
    lf, fc = logf_cumsum(misc, _forget_bias_row(lw["b_f"]))
    cn = _cols(fc, MISC_FA, N_HEADS)
    cnrow = jnp.pad(cn.transpose(0, 2, 1), ((0, 0), (0, 0), (0, CHUNK - t)))
    ya = fox_sample(page_table, layer, _cols(p32, OFF_QA, MIX_W), _cols(p32, OFF_KA, MIX_W), _cols(p32, OFF_VA, MIX_W),
                    cn, cnrow, ca_k, ca_v, ca_lft)
    yb = sb_sample(page_table, layer, _cols(p32, OFF_QB, MIX_W), _cols(p32, OFF_KB, MIX_W), _cols(p32, OFF_VB, MIX_W),
                   cb_k, cb_v)
    score_past, score_new = dsa_sample_scores(page_table, layer, _cols(p32, OFF_QI, MIX_W), misc, cc_idx_t)
    scores = jnp.concatenate([score_past, score_new], axis=-1).reshape(b * t, n_past + CHUNK)
    bias = dsa_sample_select(scores, t, n_past, topk).reshape(b, t, n_past + CHUNK)
    yc = dsa_sample_attend(page_table, layer, _cols(p32, OFF_QC, MIX_W), _cols(p32, OFF_KC, HEAD_DIM),
                           _cols(p32, OFF_VC, HEAD_DIM), bias, cc_k, cc_v)
    yd, h_last = rglru(p32, conv_state, h0, lw, F32)
    ys = [y.reshape(b * t, MIX_W).astype(BF16) for y in (ya, yb, yc, yd)]
    rows = _new_rows(p32, lf, b, t) + (_conv_tail(conv_state, p32), h_last.reshape(b, MIX_W))
    return _dense_tail(x2d, h16, ys, lw), rows


def kernel(x_prompt, x_sample, cache_a_k, cache_a_v, cache_a_logf, cache_b_k, cache_b_v, cache_c_k, cache_c_v, cache_c_idx_k, state_d_conv, state_d_h, page_table, meta_tokens, w_in, b_forget, conv_w, conv_b, w_rg_a, b_rg_a, w_rg_x, b_rg_x, rg_lambda, w_gate, w_branch, w_out, norm_mix, norm_ffn, w_ffn_gate, w_ffn_up, w_ffn_down, norm_final):
    depth = w_in.shape[0]
    n_p, seq_p, _ = x_prompt.shape
    n_dec, t_dec, _ = x_sample.shape
    n_pool = cache_a_k.shape[1]
    past_len = page_table.shape[1] * PAGE
    topk_p = min(TOPK_MAX, seq_p // 4)
    topk_s = min(TOPK_MAX, (past_len + t_dec) // 4)

    rows_view = lambda c: c.reshape(depth, n_pool, PAGE * N_HEADS, HEAD_DIM)
    caches = (rows_view(cache_a_k), rows_view(cache_a_v), cache_a_logf.transpose(0, 1, 3, 2),
              rows_view(cache_b_k), rows_view(cache_b_v), cache_c_k, cache_c_v, cache_c_idx_k.transpose(0, 1, 3, 2))

    xp = jnp.concatenate([jnp.broadcast_to(meta_tokens[None], (n_p, N_META, D_MODEL)), x_prompt], axis=1)
    t_p = seq_p + N_META
    xp = xp.reshape(n_p * t_p, D_MODEL)
    xs = x_sample.reshape(n_dec * t_dec, D_MODEL)

    prompt_rows, sample_rows = [], []
    for l in range(depth):
        lw = dict(w_in=_pad_w_in(w_in[l]), b_f=b_forget[l], conv_w=conv_w[l], conv_b=conv_b[l],
                  w_a=_block_diag(w_rg_a[l]).astype(BF16), b_a=b_rg_a[l],
                  w_x=_block_diag(w_rg_x[l]).astype(BF16), b_x=b_rg_x[l], lam=rg_lambda[l],
                  w_gate=w_gate[l].astype(BF16), w_branch=w_branch[l].astype(BF16), w_out=w_out[l].astype(BF16),
                  g_mix=norm_mix[l], g_ffn=norm_ffn[l], w_fg=w_ffn_gate[l].astype(BF16),
                  w_fu=w_ffn_up[l].astype(BF16), w_fd=w_ffn_down[l].astype(BF16))
        xp, rows = _prompt_layer(xp, n_p, t_p, lw, topk_p)
        prompt_rows.append(rows)
        xs, rows = _sample_layer(xs, n_dec, t_dec, lw, topk_s, l, page_table, caches, state_d_conv[l], state_d_h[l])
        sample_rows.append(rows)

    y_prompt = rmsnorm(xp, norm_final).reshape(n_p, t_p, D_MODEL)[:, N_META:]
    y_sample = rmsnorm(xs, norm_final).reshape(n_dec, t_dec, D_MODEL)
    stack = lambda rows: tuple(jnp.stack(list(r)) for r in zip(*rows))
    return (y_prompt, y_sample) + stack(prompt_rows) + stack(sample_rows)
```

```python
import functools

import jax
import jax.numpy as jnp
from jax import lax
from jax.experimental import pallas as pl
from jax.experimental.pallas import tpu as pltpu

F32 = jnp.float32
BF16 = jnp.bfloat16
I32 = jnp.int32

D_MODEL = 2048
N_META = 16
HEAD_DIM = 128
N_HEADS = 4
MIX_W = 512
IDX_HEADS = 8
IDX_DIM = 64
TOPK_MAX = 256
RG_C = 8.0
CONV_W = 4
RMS_EPS = 1e-6
PAGE = 128

LANES = 128
SUBLANES = 8
CHUNK = 128
BAND_TILES = 4
VMEM_LIMIT = 56 * 1024 * 1024

OFF_QA, OFF_KA, OFF_VA = 0, 512, 1024
OFF_QB, OFF_KB, OFF_VB = 1536, 2048, 2560
OFF_QC, OFF_QI, OFF_XD = 3072, 3584, 4096
OFF_KC, OFF_VC, OFF_MISC = 4608, 4736, 4864
MISC_KI, MISC_FA, MISC_WI = 0, 64, 68
N_PROJ = 5120

NEG_INF = float("-inf")
INT_MIN = -2 ** 31


def _cparams(*sem):
    return pltpu.CompilerParams(dimension_semantics=sem, vmem_limit_bytes=VMEM_LIMIT)


def _split3(x):
    x1 = x.astype(BF16)
    r1 = x - x1.astype(F32)
    x2 = r1.astype(BF16)
    x3 = (r1 - x2.astype(F32)).astype(BF16)
    return x1, x2, x3


def _dot01_left(m01, x):
    return sum(jnp.dot(m01, p, preferred_element_type=F32) for p in _split3(x))


def _dot01_right(x, m01):
    return sum(jnp.dot(p, m01, preferred_element_type=F32) for p in _split3(x))


def _dot_nt(a, b):
    return lax.dot_general(a, b, (((1,), (1,)), ((), ())), preferred_element_type=F32)


def _softplus(x):
    return jnp.maximum(x, 0.0) + jnp.log1p(jnp.exp(-jnp.abs(x)))


def _iota(shape, dim):
    return lax.broadcasted_iota(I32, shape, dim)


def _row_tile(m, pref):
    for t in range(min(pref, m), 7, -1):
        if m % t == 0 and t % SUBLANES == 0:
            return t
    raise ValueError(f"no row tile for {m}")


PROJ_TN = 512
ROW_TILES = tuple(off // PROJ_TN for off in (OFF_KA, OFF_VA, OFF_KB, OFF_VB))
TAIL_TILE0 = OFF_XD // PROJ_TN
N_TAIL = N_PROJ - OFF_XD


def _norm_proj_kernel(x_ref, g_ref, wt_ref, tail_ref, o16_ref, h16_ref, ka_ref, va_ref, kb_ref, vb_ref, h_scr):
    j = pl.program_id(1)
    tm = x_ref.shape[0]

    @pl.when(j == 0)
    def _():
        x = x_ref[...]
        y = x * lax.rsqrt(jnp.mean(x * x, axis=-1, keepdims=True) + RMS_EPS)
        h = (y * g_ref[...]).astype(BF16)
        h_scr[...] = h
        h16_ref[...] = h

    acc = _dot_nt(h_scr[...], wt_ref[...])
    o16_ref[...] = acc.astype(BF16)

    @pl.when(j >= TAIL_TILE0)
    def _():
        tail_ref[...] = acc

    for tile, rows_ref in zip(ROW_TILES, (ka_ref, va_ref, kb_ref, vb_ref)):
        @pl.when(j == tile)
        def _(rows_ref=rows_ref):
            for h in range(N_HEADS):
                rows_ref[pl.ds(h, tm, stride=N_HEADS), :] = acc[:, h * HEAD_DIM:(h + 1) * HEAD_DIM]


def norm_proj(x, g, wt16, tm_pref=1032):
    m, d = x.shape
    n = wt16.shape[0]
    tm = _row_tile(m, tm_pref)
    rows_spec = pl.BlockSpec((tm * N_HEADS, HEAD_DIM), lambda i, j: (i, 0))
    rows_shape = jax.ShapeDtypeStruct((m * N_HEADS, HEAD_DIM), F32)
    return pl.pallas_call(
        _norm_proj_kernel,
        grid=(m // tm, n // PROJ_TN),
        in_specs=[pl.BlockSpec((tm, d), lambda i, j: (i, 0), pipeline_mode=pl.Buffered(1)),
                  pl.BlockSpec((1, d), lambda i, j: (0, 0)),
                  pl.BlockSpec((PROJ_TN, d), lambda i, j: (j, 0))],
        out_specs=[pl.BlockSpec((tm, PROJ_TN), lambda i, j: (i, jnp.maximum(j - TAIL_TILE0, 0))),
                   pl.BlockSpec((tm, PROJ_TN), lambda i, j: (i, j)),
                   pl.BlockSpec((tm, d), lambda i, j: (i, 0)),
                   rows_spec, rows_spec, rows_spec, rows_spec],
        out_shape=[jax.ShapeDtypeStruct((m, N_TAIL), F32),
                   jax.ShapeDtypeStruct((m, n), BF16),
                   jax.ShapeDtypeStruct((m, d), BF16),
                   rows_shape, rows_shape, rows_shape, rows_shape],
        scratch_shapes=[pltpu.VMEM((tm, d), BF16)],
        compiler_params=_cparams("parallel", "arbitrary"),
        name="norm_proj",
    )(x, g.reshape(1, d), wt16)


def _logf_kernel(misc_ref, bias_ref, logf_ref, fcum_ref, pad_scr, *, t, n_chunks):
    lane = _iota((1, LANES), 1)
    live = (lane >= MISC_FA) & (lane < MISC_FA + N_HEADS)
    logf = jnp.where(live, jax.nn.log_sigmoid(misc_ref[0] + bias_ref[...]), 0.0)
    logf_ref[0] = logf
    pad_scr[pl.ds(0, t), :] = logf
    if n_chunks * CHUNK > t:
        pad_scr[pl.ds(t, n_chunks * CHUNK - t), :] = jnp.zeros((n_chunks * CHUNK - t, LANES), F32)
    tril = (_iota((CHUNK, CHUNK), 1) <= _iota((CHUNK, CHUNK), 0)).astype(BF16)
    carry = jnp.zeros((1, LANES), F32)
    for c in range(n_chunks):
        f = _dot01_left(tril, pad_scr[pl.ds(c * CHUNK, CHUNK), :]) + carry
        rows = min(CHUNK, t - c * CHUNK)
        fcum_ref[0, pl.ds(c * CHUNK, rows), :] = f[:rows]
        carry = f[CHUNK - 1:CHUNK, :]


def logf_cumsum(misc, bias_row):
    b, t, _ = misc.shape
    n_chunks = pl.cdiv(t, CHUNK)
    return pl.pallas_call(
        functools.partial(_logf_kernel, t=t, n_chunks=n_chunks),
        grid=(b,),
        in_specs=[pl.BlockSpec((1, t, LANES), lambda i: (i, 0, 0)),
                  pl.BlockSpec((1, LANES), lambda i: (0, 0))],
        out_specs=[pl.BlockSpec((1, t, LANES), lambda i: (i, 0, 0)),
                   pl.BlockSpec((1, t, LANES), lambda i: (i, 0, 0))],
        out_shape=[jax.ShapeDtypeStruct((b, t, LANES), F32)] * 2,
        scratch_shapes=[pltpu.VMEM((n_chunks * CHUNK, LANES), F32)],
        compiler_params=_cparams("parallel"),
        name="logf_cumsum",
    )(misc, bias_row)


def _pad_copy(dst, src, t):
    rows = dst.shape[0]
    dst[pl.ds(0, t), :] = src
    if rows > t:
        dst[pl.ds(t, rows - t), :] = jnp.zeros((rows - t, dst.shape[1]), dst.dtype)


def _stage_heads(dst, src_ref, t):
    for h in range(N_HEADS):
        _pad_copy(dst.at[h], src_ref[0, :, h * HEAD_DIM:(h + 1) * HEAD_DIM], t)


def _unstage_heads(o_ref, src, t):
    for h in range(N_HEADS):
        o_ref[0, :, h * HEAD_DIM:(h + 1) * HEAD_DIM] = src[h, 0:t, :]


def _bands(n_tiles):
    return [(lo, min(lo + BAND_TILES, n_tiles)) for lo in range(0, n_tiles, BAND_TILES)]


def _for_heads_and_tiles(n_tiles, tile):
    for lo, hi in _bands(n_tiles):
        def per_head(h, _, lo=lo, hi=hi):
            def per_tile(i, _):
                tile(h, i, hi * CHUNK)
                return 0
            return lax.fori_loop(lo, hi, per_tile, 0)
        lax.fori_loop(0, N_HEADS, per_head, 0)


def _fox_prompt_kernel(q_ref, k_ref, v_ref, frow_ref, o_ref, q_scr, k_scr, v_scr, o_scr, *, t):
    scale = HEAD_DIM ** -0.5
    _stage_heads(q_scr, q_ref, t)
    _stage_heads(k_scr, k_ref, t)
    _stage_heads(v_scr, v_ref, t)

    def tile(h, i, width):
        row0 = pl.multiple_of(i * CHUNK, CHUNK)
        q = q_scr[h, pl.ds(row0, CHUNK), :]
        s = _dot_nt(q, k_scr[h, 0:width, :]) * scale - frow_ref[0, h, :, 0:width]
        vis = _iota((CHUNK, width), 1) <= row0 + _iota((CHUNK, width), 0)
        s = jnp.where(vis, s, NEG_INF)
        p = jnp.exp(s - jnp.max(s, axis=1, keepdims=True))
        l = jnp.sum(p, axis=1, keepdims=True)
        o = jnp.dot(p.astype(BF16), v_scr[h, 0:width, :], preferred_element_type=F32) / l
        o_scr[h, pl.ds(row0, CHUNK), :] = o.astype(o_scr.dtype)

    _for_heads_and_tiles(k_scr.shape[1] // CHUNK, tile)
    _unstage_heads(o_ref, o_scr, t)


def _head_scratch(tp, n):
    return [pltpu.VMEM((N_HEADS, tp, HEAD_DIM), BF16)] * n


def fox_prompt(proj16, frow):
    b, t, _ = proj16.shape
    tp = frow.shape[-1]
    blk = lambda off: pl.BlockSpec((1, t, MIX_W), lambda i, o=off // MIX_W: (i, 0, o))
    return pl.pallas_call(
        functools.partial(_fox_prompt_kernel, t=t),
        grid=(b,),
        in_specs=[blk(OFF_QA), blk(OFF_KA), blk(OFF_VA),
                  pl.BlockSpec((1, N_HEADS, 1, tp), lambda i: (i, 0, 0, 0))],
        out_specs=pl.BlockSpec((1, t, MIX_W), lambda i: (i, 0, 0)),
        out_shape=jax.ShapeDtypeStruct((b, t, MIX_W), BF16),
        scratch_shapes=_head_scratch(tp, 4),
        compiler_params=_cparams("parallel"),
        name="fox_prompt",
    )(proj16, proj16, proj16, frow)


def _suffix_matrix():
    return (_iota((CHUNK, CHUNK), 0) > _iota((CHUNK, CHUNK), 1)).astype(BF16)


def _sb_prompt_kernel(q_ref, k_ref, v_ref, o_ref, q_scr, k_scr, v_scr, o_scr, *, t):
    scale = HEAD_DIM ** -0.5
    _stage_heads(q_scr, q_ref, t)
    _stage_heads(k_scr, k_ref, t)
    _stage_heads(v_scr, v_ref, t)
    suffix = _suffix_matrix()

    def tile(h, i, width):
        row0 = pl.multiple_of(i * CHUNK, CHUNK)
        q = q_scr[h, pl.ds(row0, CHUNK), :]
        z = _dot_nt(q, k_scr[h, 0:width, :]) * scale
        vis = _iota((CHUNK, width), 1) < row0 + _iota((CHUNK, width), 0)
        log_keep = jnp.where(vis, -_softplus(z), 0.0)
        later_chunks = jnp.zeros((CHUNK, 1), F32)
        pieces = [None] * (width // CHUNK)
        for c in reversed(range(width // CHUNK)):
            cs = slice(c * CHUNK, (c + 1) * CHUNK)
            later = _dot01_right(log_keep[:, cs], suffix) + later_chunks
            att = jnp.where(vis[:, cs], jnp.exp(log_keep[:, cs] + z[:, cs] + later), 0.0)
            pieces[c] = att.astype(BF16)
            later_chunks = later_chunks + jnp.sum(log_keep[:, cs], axis=1, keepdims=True)
        att = jnp.concatenate(pieces, axis=1)
        o = jnp.dot(att, v_scr[h, 0:width, :], preferred_element_type=F32)
        o_scr[h, pl.ds(row0, CHUNK), :] = o.astype(o_scr.dtype)

    _for_heads_and_tiles(k_scr.shape[1] // CHUNK, tile)
    _unstage_heads(o_ref, o_scr, t)


def sb_prompt(proj16):
    b, t, _ = proj16.shape
    tp = pl.cdiv(t, CHUNK) * CHUNK
    blk = lambda off: pl.BlockSpec((1, t, MIX_W), lambda i, o=off // MIX_W: (i, 0, o))
    return pl.pallas_call(
        functools.partial(_sb_prompt_kernel, t=t),
        grid=(b,),
        in_specs=[blk(OFF_QB), blk(OFF_KB), blk(OFF_VB)],
        out_specs=pl.BlockSpec((1, t, MIX_W), lambda i: (i, 0, 0)),
        out_shape=jax.ShapeDtypeStruct((b, t, MIX_W), BF16),
        scratch_shapes=_head_scratch(tp, 4),
        compiler_params=_cparams("parallel"),
        name="sb_prompt",
    )(proj16, proj16, proj16)


def _order_key(score):
    score = jnp.where(score == 0.0, 0.0, score)
    bits = lax.bitcast_convert_type(score, I32)
    return bits ^ ((bits >> 31) & 0x7FFFFFFF)


def _select_topk(key_ref, bias_ref, vis_fn, rows, width, topk):
    kf = float(topk)

    def count(pred):
        return jnp.sum(jnp.where(pred(key_ref[0:rows, 0:width]), 1.0, 0.0), axis=1, keepdims=True)

    thr0 = jnp.where(count(lambda k: k >= 0) >= kf, 0, INT_MIN).astype(I32)

    def bit_step(b, thr):
        cand = thr | (jnp.int32(1) << (30 - b))
        return jnp.where(count(lambda k: k >= cand) >= kf, cand, thr)

    thr = lax.fori_loop(0, 31, bit_step, thr0)
    need = kf - count(lambda k: k > thr)
    keys = key_ref[0:rows, 0:width]
    vis = vis_fn(0, width)
    n_tie = jnp.sum(jnp.where((keys == thr) & vis, 1.0, 0.0), axis=1, keepdims=True)
    bias_ref[0:rows, 0:width] = jnp.where((keys >= thr) & vis, 0.0, NEG_INF)

    @pl.when(jnp.max(n_tie - need) > 0.0)
    def _():
        prefix = (_iota((CHUNK, CHUNK), 0) <= _iota((CHUNK, CHUNK), 1)).astype(BF16)
        seen = jnp.zeros((rows, 1), F32)
        for c in range(width // CHUNK):
            k = key_ref[0:rows, c * CHUNK:(c + 1) * CHUNK]
            v = vis_fn(c * CHUNK, CHUNK)
            tie = jnp.where((k == thr) & v, 1.0, 0.0)
            rank = jnp.dot(tie.astype(BF16), prefix, preferred_element_type=F32) + seen
            sel = ((k > thr) & v) | ((tie > 0.0) & (rank <= need))
            bias_ref[0:rows, c * CHUNK:(c + 1) * CHUNK] = jnp.where(sel, 0.0, NEG_INF)
            seen = seen + jnp.sum(tie, axis=1, keepdims=True)


def _stack_heads(q):
    return jnp.concatenate([q[:, h * HEAD_DIM:(h + 1) * HEAD_DIM] for h in range(N_HEADS)], axis=0)


def _dsa_prompt_kernel(q_ref, qi_ref, kc_ref, vc_ref, misc32_ref, misc16_ref, o_ref,
                       kc_scr, vc_scr, ki_scr, key_scr, bias_scr, *, t, topk):
    n_full, tail = t // CHUNK, t % CHUNK
    scale = HEAD_DIM ** -0.5
    idx_scale = (IDX_DIM * IDX_HEADS) ** -0.5
    _pad_copy(kc_scr, kc_ref[0], t)
    _pad_copy(vc_scr, vc_ref[0], t)
    _pad_copy(ki_scr, misc16_ref[0], t)

    def tile(row0, tq, width):
        def vis_fn(c0, w):
            return (c0 + _iota((tq, w), 1)) <= (row0 + _iota((tq, w), 0))

        qi = qi_ref[0, pl.ds(row0, tq), :]
        wi = misc32_ref[0, pl.ds(row0, tq), MISC_WI:MISC_WI + IDX_HEADS]
        ki = ki_scr[0:width, MISC_KI:MISC_KI + IDX_DIM]
        score = None
        for h in range(IDX_HEADS):
            term = wi[:, h:h + 1] * jnp.maximum(_dot_nt(qi[:, h * IDX_DIM:(h + 1) * IDX_DIM], ki), 0.0)
            score = term if score is None else score + term
        key_scr[0:tq, 0:width] = jnp.where(vis_fn(0, width), _order_key(score * idx_scale), INT_MIN)
        _select_topk(key_scr, bias_scr, vis_fn, tq, width, topk)

        q4 = _stack_heads(q_ref[0, pl.ds(row0, tq), :])
        bias = bias_scr[0:tq, 0:width]
        s = _dot_nt(q4, kc_scr[0:width, :]) * scale + jnp.concatenate([bias] * N_HEADS, axis=0)
        p = jnp.exp(s - jnp.max(s, axis=1, keepdims=True))
        l = jnp.sum(p, axis=1, keepdims=True)
        out = jnp.dot(p.astype(BF16), vc_scr[0:width, :], preferred_element_type=F32) / l
        for h in range(N_HEADS):
            o_ref[0, pl.ds(row0, tq), h * HEAD_DIM:(h + 1) * HEAD_DIM] = out[h * tq:(h + 1) * tq].astype(o_ref.dtype)

    for lo, hi in _bands(n_full):
        def body(i, _, hi=hi):
            tile(pl.multiple_of(i * CHUNK, CHUNK), CHUNK, hi * CHUNK)
            return 0
        lax.fori_loop(lo, hi, body, 0)
    if tail:
        tile(n_full * CHUNK, tail, (n_full + 1) * CHUNK)


def dsa_prompt(tail32, proj16, topk):
    b, t, _ = proj16.shape
    tp = pl.cdiv(t, CHUNK) * CHUNK
    wide = lambda off: pl.BlockSpec((1, t, MIX_W), lambda i, o=off // MIX_W: (i, 0, o))
    narrow = lambda off: pl.BlockSpec((1, t, LANES), lambda i, o=off // LANES: (i, 0, o))
    return pl.pallas_call(
        functools.partial(_dsa_prompt_kernel, t=t, topk=topk),
        grid=(b,),
        in_specs=[wide(OFF_QC), wide(OFF_QI), narrow(OFF_KC), narrow(OFF_VC), narrow(OFF_MISC - OFF_XD), narrow(OFF_MISC)],
        out_specs=pl.BlockSpec((1, t, MIX_W), lambda i: (i, 0, 0)),
        out_shape=jax.ShapeDtypeStruct((b, t, MIX_W), BF16),
        scratch_shapes=[pltpu.VMEM((tp, HEAD_DIM), BF16), pltpu.VMEM((tp, HEAD_DIM), BF16),
                        pltpu.VMEM((tp, LANES), BF16),
                        pltpu.VMEM((CHUNK, tp), I32), pltpu.VMEM((CHUNK, tp), F32)],
        compiler_params=_cparams("parallel"),
        name="dsa_prompt",
    )(proj16, proj16, proj16, proj16, tail32, proj16)


CONV_PAD = 8


def _rglru_kernel(xd_ref, cs_ref, h0_ref, cw_ref, cb_ref, wa_ref, ba_ref, wx_ref, bx_ref, lam_ref,
                  y_ref, hl_ref, xpad_scr, a_scr, u_scr, hs_scr, *, t):
    n_full, tail = t // CHUNK, t % CHUNK
    xpad_scr[pl.ds(0, CONV_PAD), :] = jnp.zeros((CONV_PAD, MIX_W), F32)
    xpad_scr[pl.ds(CONV_PAD - (CONV_W - 1), CONV_W - 1), :] = cs_ref[0]
    xpad_scr[pl.ds(CONV_PAD, t), :] = xd_ref[0]
    decay_rate = -RG_C * _softplus(-lam_ref[...])

    def chunk(r0, rows, h):
        win = xpad_scr[pl.ds(r0, rows + CONV_PAD), :]
        xc = cb_ref[...]
        for i in range(CONV_W):
            lo = CONV_PAD - (CONV_W - 1) + i
            xc = xc + win[lo:lo + rows] * cw_ref[i:i + 1, :]
        xc16 = xc.astype(BF16)
        gate_r = jax.nn.sigmoid(jnp.dot(xc16, wa_ref[...], preferred_element_type=F32) + ba_ref[...])
        gate_i = jax.nn.sigmoid(jnp.dot(xc16, wx_ref[...], preferred_element_type=F32) + bx_ref[...])
        log_a = gate_r * decay_rate
        a_scr[0:rows, :] = jnp.exp(log_a)
        th = jnp.tanh(log_a)
        u_scr[0:rows, :] = jnp.sqrt(-2.0 * th / (1.0 - th)) * gate_i * xc

        def step(r, hh):
            hh = a_scr[pl.ds(r, 1), :] * hh + u_scr[pl.ds(r, 1), :]
            hs_scr[pl.ds(r, 1), :] = hh
            return hh
        h = lax.fori_loop(0, rows, step, h, unroll=8)
        y_ref[0, pl.ds(r0, rows), :] = hs_scr[0:rows, :].astype(y_ref.dtype)
        return h

    h = h0_ref[0]
    if n_full:
        h = lax.fori_loop(0, n_full, lambda i, hh: chunk(pl.multiple_of(i * CHUNK, CHUNK), CHUNK, hh), h)
    if tail:
        h = chunk(n_full * CHUNK, tail, h)
    hl_ref[0] = h


def rglru(tail32, conv_state, h0, lw, out_dtype):
    b, t, _ = tail32.shape
    conv_w, conv_b, wa16, b_a, wx16, b_x, lam = (lw[k] for k in ("conv_w", "conv_b", "w_a", "b_a", "w_x", "b_x", "lam"))
    row = lambda v: v.reshape(1, MIX_W)
    const = lambda shape: pl.BlockSpec(shape, lambda i: (0,) * len(shape))
    return pl.pallas_call(
        functools.partial(_rglru_kernel, t=t),
        grid=(b,),
        in_specs=[pl.BlockSpec((1, t, MIX_W), lambda i: (i, 0, 0)),
                  pl.BlockSpec((1, CONV_W - 1, MIX_W), lambda i: (i, 0, 0)),
                  pl.BlockSpec((1, 1, MIX_W), lambda i: (i, 0, 0)),
                  const((CONV_W, MIX_W)), const((1, MIX_W)),
                  const((MIX_W, MIX_W)), const((1, MIX_W)),
                  const((MIX_W, MIX_W)), const((1, MIX_W)), const((1, MIX_W))],
        out_specs=[pl.BlockSpec((1, t, MIX_W), lambda i: (i, 0, 0)),
                   pl.BlockSpec((1, 1, MIX_W), lambda i: (i, 0, 0))],
        out_shape=[jax.ShapeDtypeStruct((b, t, MIX_W), out_dtype), jax.ShapeDtypeStruct((b, 1, MIX_W), F32)],
        scratch_shapes=[pltpu.VMEM((CONV_PAD + pl.cdiv(t, CHUNK) * CHUNK + CONV_PAD, MIX_W), F32),
                        pltpu.VMEM((CHUNK, MIX_W), F32), pltpu.VMEM((CHUNK, MIX_W), F32),
                        pltpu.VMEM((CHUNK, MIX_W), F32)],
        compiler_params=_cparams("parallel"),
        name="rglru",
    )(tail32, conv_state, h0.reshape(b, 1, MIX_W), conv_w, row(conv_b), wa16, row(b_a), wx16, row(b_x), row(lam))


def _block_diag(w):
    n, c, e = w.shape
    eye = jnp.eye(n, dtype=w.dtype)
    return (eye[:, None, :, None] * w[:, :, None, :]).reshape(n * c, n * e)


def _merge_kernel(h_ref, ya_ref, yb_ref, yc_ref, yd_ref, g0_ref, g1_ref, g2_ref, g3_ref, wb_ref, o_ref):
    h = h_ref[...]
    acc = None
    for n, (y_ref, g_ref) in enumerate(((ya_ref, g0_ref), (yb_ref, g1_ref), (yc_ref, g2_ref), (yd_ref, g3_ref))):
        gate = jax.nn.sigmoid(jnp.dot(h, g_ref[...], preferred_element_type=F32))
        term = gate * jnp.dot(y_ref[...], wb_ref[n], preferred_element_type=F32)
        acc = term if acc is None else acc + term
    o_ref[...] = acc.astype(o_ref.dtype)


def merge(h16, ys, wg16, wb16, tm_pref=512, tn=512):
    m, d = h16.shape
    tm = _row_tile(m, tm_pref)
    nj = d // tn
    y_spec = pl.BlockSpec((tm, MIX_W), lambda i, j: (i, 0))
    g_spec = lambda n: pl.BlockSpec((d, tn), lambda i, j, n=n: (0, n * nj + j))
    return pl.pallas_call(
        _merge_kernel,
        grid=(m // tm, nj),
        in_specs=[pl.BlockSpec((tm, d), lambda i, j: (i, 0)), y_spec, y_spec, y_spec, y_spec,
                  g_spec(0), g_spec(1), g_spec(2), g_spec(3),
                  pl.BlockSpec((4, MIX_W, tn), lambda i, j: (0, 0, j))],
        out_specs=pl.BlockSpec((tm, tn), lambda i, j: (i, j)),
        out_shape=jax.ShapeDtypeStruct((m, d), BF16),
        compiler_params=_cparams("parallel", "arbitrary"),
        name="merge",
    )(h16, *ys, wg16, wg16, wg16, wg16, wb16)


def _matmul_res_kernel(a_ref, w_ref, x_ref, o_ref):
    o_ref[...] = x_ref[...] + jnp.dot(a_ref[...], w_ref[...], preferred_element_type=F32)


def matmul_residual(a16, w16, x, tm_pref=344):
    m, k = a16.shape
    n = w16.shape[1]
    tm = _row_tile(m, tm_pref)
    return pl.pallas_call(
        _matmul_res_kernel,
        grid=(m // tm,),
        in_specs=[pl.BlockSpec((tm, k), lambda i: (i, 0)),
                  pl.BlockSpec((k, n), lambda i: (0, 0)),
                  pl.BlockSpec((tm, n), lambda i: (i, 0))],
        out_specs=pl.BlockSpec((tm, n), lambda i: (i, 0)),
        out_shape=jax.ShapeDtypeStruct((m, n), F32),
        compiler_params=_cparams("parallel"),
        name="out_proj",
    )(a16, w16, x)


def _ffn_kernel(x_ref, g_ref, wg_ref, wu_ref, wd_ref, o_ref, h_scr, acc_scr):
    f = pl.program_id(1)

    @pl.when(f == 0)
    def _():
        x = x_ref[...]
        y = x * lax.rsqrt(jnp.mean(x * x, axis=-1, keepdims=True) + RMS_EPS)
        h_scr[...] = (y * g_ref[...]).astype(BF16)
        acc_scr[...] = jnp.zeros_like(acc_scr)

    h = h_scr[...]
    gate = jnp.dot(h, wg_ref[...], preferred_element_type=F32)
    up = jnp.dot(h, wu_ref[...], preferred_element_type=F32)
    act = (jax.nn.silu(gate) * up).astype(BF16)
    acc_scr[...] += jnp.dot(act, wd_ref[...], preferred_element_type=F32)

    @pl.when(f == pl.num_programs(1) - 1)
    def _():
        o_ref[...] = x_ref[...] + acc_scr[...]


def ffn(x, g, wg16, wu16, wd16, tm_pref=688, tf=512):
    m, d = x.shape
    dff = wg16.shape[1]
    tm = _row_tile(m, tm_pref)
    return pl.pallas_call(
        _ffn_kernel,
        grid=(m // tm, dff // tf),
        in_specs=[pl.BlockSpec((tm, d), lambda i, f: (i, 0)),
                  pl.BlockSpec((1, d), lambda i, f: (0, 0)),
                  pl.BlockSpec((d, tf), lambda i, f: (0, f)),
                  pl.BlockSpec((d, tf), lambda i, f: (0, f)),
                  pl.BlockSpec((tf, d), lambda i, f: (f, 0))],
        out_specs=pl.BlockSpec((tm, d), lambda i, f: (i, 0)),
        out_shape=jax.ShapeDtypeStruct((m, d), F32),
        scratch_shapes=[pltpu.VMEM((tm, d), BF16), pltpu.VMEM((tm, d), F32)],
        compiler_params=_cparams("parallel", "arbitrary"),
        name="ffn",
    )(x, g.reshape(1, d), wg16, wu16, wd16)


def _rmsnorm_kernel(x_ref, g_ref, o_ref):
    x = x_ref[...]
    o_ref[...] = x * lax.rsqrt(jnp.mean(x * x, axis=-1, keepdims=True) + RMS_EPS) * g_ref[...]


def rmsnorm(x, g, tm_pref=512):
    m, d = x.shape
    tm = _row_tile(m, tm_pref)
    return pl.pallas_call(
        _rmsnorm_kernel,
        grid=(m // tm,),
        in_specs=[pl.BlockSpec((tm, d), lambda i: (i, 0)), pl.BlockSpec((1, d), lambda i: (0, 0))],
        out_specs=pl.BlockSpec((tm, d), lambda i: (i, 0)),
        out_shape=jax.ShapeDtypeStruct((m, d), F32),
        compiler_params=_cparams("parallel"),
        name="final_norm",
    )(x, g.reshape(1, d))


PAGES_PER_STEP = 16
STEP_KEYS = PAGES_PER_STEP * PAGE


def _page_specs(block, layer, n_pages, reverse):
    def spec(p):
        def index(b, s, pt):
            j = s * PAGES_PER_STEP + p
            if reverse:
                j = n_pages - 1 - j
            return (layer, pt[b, j], 0, 0)
        return pl.BlockSpec((1, 1) + block, index)
    return [spec(p) for p in range(PAGES_PER_STEP)]


def _per_seq(shape):
    return pl.BlockSpec((1,) + shape, lambda b, s, pt: (b,) + (0,) * len(shape))


def _head_page(ref, h):
    return ref.at[0, 0][pl.ds(h, PAGE, stride=N_HEADS), :].astype(BF16)


def _own_head_rows(per_head, t):
    return jnp.concatenate([per_head[h][h * t:(h + 1) * t] for h in range(N_HEADS)], axis=0)


def _query_index(t, width):
    return jnp.concatenate([_iota((t, width), 0)] * N_HEADS, axis=0)


def _pad_rows(dst, src):
    dst[...] = jnp.zeros(dst.shape, dst.dtype)
    dst[pl.ds(0, src.shape[0]), :] = src


def _per_head_rows(x, t):
    return jnp.concatenate([jnp.broadcast_to(x[h:h + 1], (t, x.shape[1])) for h in range(N_HEADS)], axis=0)


def _per_head_cols(x):
    return jnp.concatenate([x[:, h:h + 1] for h in range(N_HEADS)], axis=0)


def _write_stacked(o_ref, out, t):
    for h in range(N_HEADS):
        o_ref[0, :, h * HEAD_DIM:(h + 1) * HEAD_DIM] = out[h * t:(h + 1) * t].astype(o_ref.dtype)


def _softmax_update(carry, s, pv_fn):
    m, l, acc = carry
    m_new = jnp.maximum(m, jnp.max(s, axis=1, keepdims=True))
    m_safe = jnp.where(m_new == NEG_INF, 0.0, m_new)
    alpha = jnp.exp(m - m_safe)
    p = jnp.exp(s - m_safe)
    l = alpha * l + jnp.sum(p, axis=1, keepdims=True)
    return m_new, l, alpha * acc + pv_fn(p.astype(BF16))


def _softmax_init(rows):
    return (jnp.full((rows, 1), NEG_INF, F32), jnp.zeros((rows, 1), F32), jnp.zeros((rows, HEAD_DIM), F32))


def _new_keys_per_head(q4, pad_ref, t):
    return _own_head_rows([_dot_nt(q4, pad_ref[:, h * HEAD_DIM:(h + 1) * HEAD_DIM].astype(BF16))
                           for h in range(N_HEADS)], t)


def _new_values_per_head(p16, pad_ref, t):
    return _own_head_rows([jnp.dot(p16, pad_ref[:, h * HEAD_DIM:(h + 1) * HEAD_DIM].astype(BF16),
                                   preferred_element_type=F32) for h in range(N_HEADS)], t)


def _page_scores(q4, k_refs, t):
    return _own_head_rows([_dot_nt(q4, jnp.concatenate([_head_page(k_ref, h) for k_ref in k_refs], axis=0))
                           for h in range(N_HEADS)], t)


def _page_values(p16, v_refs, t):
    return _own_head_rows([jnp.dot(p16, jnp.concatenate([_head_page(v_ref, h) for v_ref in v_refs], axis=0),
                                   preferred_element_type=F32) for h in range(N_HEADS)], t)


def _fox_sample_kernel(pt_ref, q_ref, kn_ref, vn_ref, cn_ref, cnrow_ref, *rest, t):
    n = PAGES_PER_STEP
    k_refs, v_refs, lf_refs = rest[0:n], rest[n:2 * n], rest[2 * n:3 * n]
    o_ref, kpad, vpad, m_scr, l_scr, acc_scr, d_scr = rest[3 * n:]
    s_id = pl.program_id(1)
    scale = HEAD_DIM ** -0.5
    rows = N_HEADS * t
    q4 = _stack_heads(q_ref[0]).astype(BF16)
    cn = _per_head_cols(cn_ref[0])

    @pl.when(s_id == 0)
    def _():
        _pad_rows(kpad, kn_ref[0])
        _pad_rows(vpad, vn_ref[0])
        s = _new_keys_per_head(q4, kpad, t) * scale + cn - _per_head_rows(cnrow_ref[0], t)
        col = _iota((rows, CHUNK), 1)
        s = jnp.where((col <= _query_index(t, CHUNK)) & (col < t), s, NEG_INF)
        m, l, acc = _softmax_update(_softmax_init(rows), s, lambda p16: _new_values_per_head(p16, vpad, t))
        m_scr[...], l_scr[...], acc_scr[...] = m, l, acc
        d_scr[...] = jnp.zeros(d_scr.shape, F32)

    suffix = _suffix_matrix()
    later = d_scr[...]
    decays = []
    for p in range(n):
        lf = jnp.concatenate([lf_refs[p][0, 0], jnp.zeros((SUBLANES - N_HEADS, PAGE), F32)], axis=0)
        decays.append(_dot01_right(lf, suffix) + later)
        later = later + jnp.sum(lf, axis=1, keepdims=True)
    decay = jnp.concatenate(decays, axis=1)
    s = _page_scores(q4, k_refs, t) * scale + _per_head_rows(decay[0:N_HEADS], t) + cn
    carry = _softmax_update((m_scr[...], l_scr[...], acc_scr[...]), s, lambda p16: _page_values(p16, v_refs, t))
    m_scr[...], l_scr[...], acc_scr[...] = carry
    d_scr[...] = later

    @pl.when(s_id == pl.num_programs(1) - 1)
    def _():
        _write_stacked(o_ref, carry[2] / carry[1], t)


def _kv_page_block():
    return (PAGE * N_HEADS, HEAD_DIM)


def fox_sample(page_table, layer, q, kn, vn, cn, cnrow, cache_k, cache_v, cache_lft):
    b, t, _ = q.shape
    n_pages = page_table.shape[1]
    rows = N_HEADS * t
    grid_spec = pltpu.PrefetchScalarGridSpec(
        num_scalar_prefetch=1,
        grid=(b, n_pages // PAGES_PER_STEP),
        in_specs=[_per_seq((t, MIX_W)), _per_seq((t, MIX_W)), _per_seq((t, MIX_W)),
                  _per_seq((t, N_HEADS)), _per_seq((N_HEADS, CHUNK))]
                 + _page_specs(_kv_page_block(), layer, n_pages, True)
                 + _page_specs(_kv_page_block(), layer, n_pages, True)
                 + _page_specs((N_HEADS, PAGE), layer, n_pages, True),
        out_specs=_per_seq((t, MIX_W)),
        scratch_shapes=[pltpu.VMEM((CHUNK, MIX_W), F32), pltpu.VMEM((CHUNK, MIX_W), F32),
                        pltpu.VMEM((rows, 1), F32), pltpu.VMEM((rows, 1), F32), pltpu.VMEM((rows, HEAD_DIM), F32),
                        pltpu.VMEM((SUBLANES, 1), F32)])
    return pl.pallas_call(
        functools.partial(_fox_sample_kernel, t=t),
        grid_spec=grid_spec,
        out_shape=jax.ShapeDtypeStruct((b, t, MIX_W), F32),
        compiler_params=_cparams("parallel", "arbitrary"),
        name="fox_sample",
    )(page_table, q, kn, vn, cn, cnrow, *([cache_k] * PAGES_PER_STEP), *([cache_v] * PAGES_PER_STEP),
      *([cache_lft] * PAGES_PER_STEP))


def _sb_sample_kernel(pt_ref, q_ref, kn_ref, vn_ref, *rest, t):
    n = PAGES_PER_STEP
    k_refs, v_refs = rest[0:n], rest[n:2 * n]
    o_ref, kpad, vpad, r_scr, acc_scr = rest[2 * n:]
    s_id = pl.program_id(1)
    scale = HEAD_DIM ** -0.5
    rows = N_HEADS * t
    q4 = _stack_heads(q_ref[0]).astype(BF16)
    suffix = _suffix_matrix()

    @pl.when(s_id == 0)
    def _():
        _pad_rows(kpad, kn_ref[0])
        _pad_rows(vpad, vn_ref[0])
        z = _new_keys_per_head(q4, kpad, t) * scale
        vis = _iota((rows, CHUNK), 1) < _query_index(t, CHUNK)
        log_keep = jnp.where(vis, -_softplus(z), 0.0)
        att = jnp.where(vis, jnp.exp(log_keep + z + _dot01_right(log_keep, suffix)), 0.0)
        acc_scr[...] = _new_values_per_head(att.astype(BF16), vpad, t)
        r_scr[...] = jnp.sum(log_keep, axis=1, keepdims=True)

    later = r_scr[...]
    z = _page_scores(q4, k_refs, t) * scale
    log_keep = -_softplus(z)
    page = lambda x, p: x[:, p * PAGE:(p + 1) * PAGE]
    in_page = _dot01_right(jnp.concatenate([page(log_keep, p) for p in range(n)], axis=0), suffix)
    pieces = []
    for p in range(n):
        suf = in_page[p * rows:(p + 1) * rows]
        pieces.append(jnp.exp(page(log_keep, p) + page(z, p) + suf + later).astype(BF16))
        later = later + suf[:, 0:1] + page(log_keep, p)[:, 0:1]
    acc = acc_scr[...] + _page_values(jnp.concatenate(pieces, axis=1), v_refs, t)
    r_scr[...], acc_scr[...] = later, acc

    @pl.when(s_id == pl.num_programs(1) - 1)
    def _():
        _write_stacked(o_ref, acc, t)


def sb_sample(page_table, layer, q, kn, vn, cache_k, cache_v):
    b, t, _ = q.shape
    n_pages = page_table.shape[1]
    rows = N_HEADS * t
    grid_spec = pltpu.PrefetchScalarGridSpec(
        num_scalar_prefetch=1,
        grid=(b, n_pages // PAGES_PER_STEP),
        in_specs=[_per_seq((t, MIX_W))] * 3
                 + _page_specs(_kv_page_block(), layer, n_pages, True)
                 + _page_specs(_kv_page_block(), layer, n_pages, True),
        out_specs=_per_seq((t, MIX_W)),
        scratch_shapes=[pltpu.VMEM((CHUNK, MIX_W), F32), pltpu.VMEM((CHUNK, MIX_W), F32),
                        pltpu.VMEM((rows, 1), F32), pltpu.VMEM((rows, HEAD_DIM), F32)])
    return pl.pallas_call(
        functools.partial(_sb_sample_kernel, t=t),
        grid_spec=grid_spec,
        out_shape=jax.ShapeDtypeStruct((b, t, MIX_W), F32),
        compiler_params=_cparams("parallel", "arbitrary"),
        name="sb_sample",
    )(page_table, q, kn, vn, *([cache_k] * PAGES_PER_STEP), *([cache_v] * PAGES_PER_STEP))


def _weighted_relu_sum(g, wi, t):
    score = None
    for h in range(IDX_HEADS):
        term = wi[:, h:h + 1] * jnp.maximum(g[h * t:(h + 1) * t], 0.0)
        score = term if score is None else score + term
    return score * ((IDX_DIM * IDX_HEADS) ** -0.5)


def _dsa_scores_kernel(pt_ref, qi_ref, misc_ref, *rest, t):
    n = PAGES_PER_STEP
    idx_refs = rest[0:n]
    past_ref, new_ref, kpad = rest[n:]
    qi = qi_ref[0]
    qi_stack = jnp.concatenate([qi[:, h * IDX_DIM:(h + 1) * IDX_DIM] for h in range(IDX_HEADS)], axis=0).astype(BF16)
    wi = misc_ref[0, :, MISC_WI:MISC_WI + IDX_HEADS]

    @pl.when(pl.program_id(1) == 0)
    def _():
        _pad_rows(kpad, misc_ref[0])
        g = _dot_nt(qi_stack, kpad[:, MISC_KI:MISC_KI + IDX_DIM].astype(BF16))
        new_ref[0] = _weighted_relu_sum(g, wi, t)

    for p in range(n):
        g = jnp.dot(qi_stack, idx_refs[p][0, 0].astype(BF16), preferred_element_type=F32)
        past_ref[0, :, p * PAGE:(p + 1) * PAGE] = _weighted_relu_sum(g, wi, t)


def dsa_sample_scores(page_table, layer, qi, misc, cache_idx_t):
    b, t, _ = qi.shape
    n_pages = page_table.shape[1]
    grid_spec = pltpu.PrefetchScalarGridSpec(
        num_scalar_prefetch=1,
        grid=(b, n_pages // PAGES_PER_STEP),
        in_specs=[_per_seq((t, MIX_W)), _per_seq((t, LANES))] + _page_specs((IDX_DIM, PAGE), layer, n_pages, False),
        out_specs=[pl.BlockSpec((1, t, STEP_KEYS), lambda i, s, pt: (i, 0, s)), _per_seq((t, CHUNK))],
        scratch_shapes=[pltpu.VMEM((CHUNK, LANES), F32)])
    return pl.pallas_call(
        functools.partial(_dsa_scores_kernel, t=t),
        grid_spec=grid_spec,
        out_shape=[jax.ShapeDtypeStruct((b, t, n_pages * PAGE), F32), jax.ShapeDtypeStruct((b, t, CHUNK), F32)],
        compiler_params=_cparams("parallel", "arbitrary"),
        name="dsa_sample_scores",
    )(page_table, qi, misc, *([cache_idx_t] * PAGES_PER_STEP))


SELECT_ROWS = 64


def _dsa_select_kernel(score_ref, bias_ref, key_scr, *, t, n_past, topk):
    rows, width = score_ref.shape
    qidx = jnp.concatenate([_iota((t, 1), 0)] * (rows // t), axis=0)

    def vis_fn(c0, w):
        col = c0 + _iota((rows, w), 1)
        return (col < n_past) | ((col - n_past <= qidx) & (col - n_past < t))

    key_scr[...] = jnp.where(vis_fn(0, width), _order_key(score_ref[...]), INT_MIN)
    _select_topk(key_scr, bias_ref, vis_fn, rows, width, topk)


def dsa_sample_select(scores, t, n_past, topk):
    m, width = scores.shape
    rows = _row_tile(m, SELECT_ROWS)
    assert rows % t == 0
    spec = pl.BlockSpec((rows, width), lambda i: (i, 0))
    return pl.pallas_call(
        functools.partial(_dsa_select_kernel, t=t, n_past=n_past, topk=topk),
        grid=(m // rows,),
        in_specs=[spec],
        out_specs=spec,
        out_shape=jax.ShapeDtypeStruct(scores.shape, F32),
        scratch_shapes=[pltpu.VMEM((rows, width), I32)],
        compiler_params=_cparams("parallel"),
        name="dsa_sample_select",
    )(scores)


def _dsa_attend_kernel(pt_ref, q_ref, kn_ref, vn_ref, bias_new_ref, bias_past_ref, *rest, t):
    n = PAGES_PER_STEP
    k_refs, v_refs = rest[0:n], rest[n:2 * n]
    o_ref, kpad, vpad, m_scr, l_scr, acc_scr = rest[2 * n:]
    s_id = pl.program_id(1)
    scale = HEAD_DIM ** -0.5
    rows = N_HEADS * t
    q4 = _stack_heads(q_ref[0]).astype(BF16)

    @pl.when(s_id == 0)
    def _():
        _pad_rows(kpad, kn_ref[0])
        _pad_rows(vpad, vn_ref[0])
        s = _dot_nt(q4, kpad[...].astype(BF16)) * scale + jnp.concatenate([bias_new_ref[0]] * N_HEADS, axis=0)
        m, l, acc = _softmax_update(
            _softmax_init(rows), s,
            lambda p16: jnp.dot(p16, vpad[...].astype(BF16), preferred_element_type=F32))
        m_scr[...], l_scr[...], acc_scr[...] = m, l, acc

    step_rows = lambda refs: jnp.concatenate([r[0, 0].astype(BF16) for r in refs], axis=0)
    s = _dot_nt(q4, step_rows(k_refs)) * scale + jnp.concatenate([bias_past_ref[0]] * N_HEADS, axis=0)
    carry = _softmax_update((m_scr[...], l_scr[...], acc_scr[...]), s,
                            lambda p16: jnp.dot(p16, step_rows(v_refs), preferred_element_type=F32))
    m_scr[...], l_scr[...], acc_scr[...] = carry

    @pl.when(s_id == pl.num_programs(1) - 1)
    def _():
        _write_stacked(o_ref, carry[2] / carry[1], t)


def dsa_sample_attend(page_table, layer, q, kn, vn, bias, cache_k, cache_v):
    b, t, _ = q.shape
    n_pages = page_table.shape[1]
    rows = N_HEADS * t
    grid_spec = pltpu.PrefetchScalarGridSpec(
        num_scalar_prefetch=1,
        grid=(b, n_pages // PAGES_PER_STEP),
        in_specs=[_per_seq((t, MIX_W)), _per_seq((t, HEAD_DIM)), _per_seq((t, HEAD_DIM)),
                  pl.BlockSpec((1, t, CHUNK), lambda i, s, pt: (i, 0, n_pages)),
                  pl.BlockSpec((1, t, STEP_KEYS), lambda i, s, pt: (i, 0, s))]
                 + _page_specs((PAGE, HEAD_DIM), layer, n_pages, False)
                 + _page_specs((PAGE, HEAD_DIM), layer, n_pages, False),
        out_specs=_per_seq((t, MIX_W)),
        scratch_shapes=[pltpu.VMEM((CHUNK, HEAD_DIM), F32), pltpu.VMEM((CHUNK, HEAD_DIM), F32),
                        pltpu.VMEM((rows, 1), F32), pltpu.VMEM((rows, 1), F32), pltpu.VMEM((rows, HEAD_DIM), F32)])
    return pl.pallas_call(
        functools.partial(_dsa_attend_kernel, t=t),
        grid_spec=grid_spec,
        out_shape=jax.ShapeDtypeStruct((b, t, MIX_W), F32),
        compiler_params=_cparams("parallel", "arbitrary"),
        name="dsa_sample_attend",
    )(page_table, q, kn, vn, bias, bias, *([cache_k] * PAGES_PER_STEP), *([cache_v] * PAGES_PER_STEP))


def _pad_w_in_t(w):
    wt = w.T
    widths = (512, 512, 512, 4, 512, 512, 512, 512, 128, 128, 512, 64, 8, 512)
    names = ("qa", "ka", "va", "fa", "qb", "kb", "vb", "qc", "kc", "vc", "qi", "ki", "wi", "xd")
    seg, off = {}, 0
    for name, width in zip(names, widths):
        seg[name] = wt[off:off + width]
        off += width
    order = ("qa", "ka", "va", "qb", "kb", "vb", "qc", "qi", "xd", "kc", "vc", "ki", "fa", "wi")
    rows = jnp.concatenate([seg[n] for n in order], axis=0)
    return jnp.pad(rows, ((0, N_PROJ - rows.shape[0]), (0, 0))).astype(BF16)


def _cols(p, off, width):
    return p[..., off:off + width]


def _tail_cols(tail32, off, width):
    return _cols(tail32, off - OFF_XD, width)


def _new_rows(tail32, head_rows, lf, b, t):
    ka, va, kb, vb = (r.reshape(b, t, N_HEADS, HEAD_DIM) for r in head_rows)
    return (ka, va, _cols(lf, MISC_FA, N_HEADS), kb, vb, _tail_cols(tail32, OFF_KC, HEAD_DIM),
            _tail_cols(tail32, OFF_VC, HEAD_DIM), _tail_cols(tail32, OFF_MISC + MISC_KI, IDX_DIM))


def _dense_tail(x2d, h16, ys, lw):
    merged = merge(h16, ys, lw["w_gate"], lw["w_branch"])
    x2d = matmul_residual(merged, lw["w_out"], x2d)
    return ffn(x2d, lw["g_ffn"], lw["w_fg"], lw["w_fu"], lw["w_fd"])


def _forget_bias_row(b_f):
    return jnp.zeros((1, LANES), F32).at[0, MISC_FA:MISC_FA + N_HEADS].set(b_f)


def _conv_tail(conv_state, tail32):
    xd = _tail_cols(tail32, OFF_XD, MIX_W)
    return jnp.concatenate([conv_state, xd], axis=1)[:, -(CONV_W - 1):]


def _prompt_layer(x2d, b, t, lw, topk):
    tail32, p16, h16, *head_rows = norm_proj(x2d, lw["g_mix"], lw["w_in_t"])
    tail32, p16 = tail32.reshape(b, t, N_TAIL), p16.reshape(b, t, N_PROJ)
    lf, fc = logf_cumsum(_tail_cols(tail32, OFF_MISC, LANES), _forget_bias_row(lw["b_f"]))
    tp = pl.cdiv(t, CHUNK) * CHUNK
    frow = jnp.pad(_cols(fc, MISC_FA, N_HEADS), ((0, 0), (0, tp - t), (0, 0))).transpose(0, 2, 1)
    conv0 = jnp.zeros((b, CONV_W - 1, MIX_W), F32)
    ya = fox_prompt(p16, frow.reshape(b, N_HEADS, 1, tp))
    yb = sb_prompt(p16)
    yc = dsa_prompt(tail32, p16, topk)
    yd, h_last = rglru(tail32, conv0, jnp.zeros((b, MIX_W), F32), lw, BF16)
    ys = [y.reshape(b * t, MIX_W) for y in (ya, yb, yc, yd)]
    rows = _new_rows(tail32, head_rows, lf, b, t) + (_conv_tail(conv0, tail32), h_last.reshape(b, MIX_W))
    return _dense_tail(x2d, h16, ys, lw), rows


def _sample_layer(x2d, b, t, lw, topk, layer, page_table, caches, conv_state, h0):
    ca_k, ca_v, ca_lft, cb_k, cb_v, cc_k, cc_v, cc_idx_t = caches
    n_past = page_table.shape[1] * PAGE
    tail32, p16, h16, *head_rows = norm_proj(x2d, lw["g_mix"], lw["w_in_t"])
    tail32 = tail32.reshape(b, t, N_TAIL)
    seg = lambda off, width: _cols(p16, off, width).astype(F32).reshape(b, t, width)
    misc = _tail_cols(tail32, OFF_MISC, LANES)
    lf, fc = logf_cumsum(misc, _forget_bias_row(lw["b_f"]))
    cn = _cols(fc, MISC_FA, N_HEADS)
    cnrow = jnp.pad(cn.transpose(0, 2, 1), ((0, 0), (0, 0), (0, CHUNK - t)))
    ya = fox_sample(page_table, layer, seg(OFF_QA, MIX_W), seg(OFF_KA, MIX_W), seg(OFF_VA, MIX_W),
                    cn, cnrow, ca_k, ca_v, ca_lft)
    yb = sb_sample(page_table, layer, seg(OFF_QB, MIX_W), seg(OFF_KB, MIX_W), seg(OFF_VB, MIX_W), cb_k, cb_v)
    score_past, score_new = dsa_sample_scores(page_table, layer, seg(OFF_QI, MIX_W), misc, cc_idx_t)
    scores = jnp.concatenate([score_past, score_new], axis=-1).reshape(b * t, n_past + CHUNK)
    bias = dsa_sample_select(scores, t, n_past, topk).reshape(b, t, n_past + CHUNK)
    yc = dsa_sample_attend(page_table, layer, seg(OFF_QC, MIX_W), seg(OFF_KC, HEAD_DIM), seg(OFF_VC, HEAD_DIM),
                           bias, cc_k, cc_v)
    yd, h_last = rglru(tail32, conv_state, h0, lw, F32)
    ys = [y.reshape(b * t, MIX_W).astype(BF16) for y in (ya, yb, yc, yd)]
    rows = _new_rows(tail32, head_rows, lf, b, t) + (_conv_tail(conv_state, tail32), h_last.reshape(b, MIX_W))
    return _dense_tail(x2d, h16, ys, lw), rows


def kernel(x_prompt, x_sample, cache_a_k, cache_a_v, cache_a_logf, cache_b_k, cache_b_v, cache_c_k, cache_c_v, cache_c_idx_k, state_d_conv, state_d_h, page_table, meta_tokens, w_in, b_forget, conv_w, conv_b, w_rg_a, b_rg_a, w_rg_x, b_rg_x, rg_lambda, w_gate, w_branch, w_out, norm_mix, norm_ffn, w_ffn_gate, w_ffn_up, w_ffn_down, norm_final):
    depth = w_in.shape[0]
    n_p, seq_p, _ = x_prompt.shape
    n_dec, t_dec, _ = x_sample.shape
    n_pool = cache_a_k.shape[1]
    past_len = page_table.shape[1] * PAGE
    topk_p = min(TOPK_MAX, seq_p // 4)
    topk_s = min(TOPK_MAX, (past_len + t_dec) // 4)

    rows_view = lambda c: c.reshape(depth, n_pool, PAGE * N_HEADS, HEAD_DIM)
    caches = (rows_view(cache_a_k), rows_view(cache_a_v), cache_a_logf.transpose(0, 1, 3, 2),
              rows_view(cache_b_k), rows_view(cache_b_v), cache_c_k, cache_c_v, cache_c_idx_k.transpose(0, 1, 3, 2))

    xp = jnp.concatenate([jnp.broadcast_to(meta_tokens[None], (n_p, N_META, D_MODEL)), x_prompt], axis=1)
    t_p = seq_p + N_META
    xp = xp.reshape(n_p * t_p, D_MODEL)
    xs = x_sample.reshape(n_dec * t_dec, D_MODEL)

    prompt_rows, sample_rows = [], []
    for l in range(depth):
        lw = dict(w_in_t=_pad_w_in_t(w_in[l]), b_f=b_forget[l], conv_w=conv_w[l], conv_b=conv_b[l],
                  w_a=_block_diag(w_rg_a[l]).astype(BF16), b_a=b_rg_a[l],
                  w_x=_block_diag(w_rg_x[l]).astype(BF16), b_x=b_rg_x[l], lam=rg_lambda[l],
                  w_gate=w_gate[l].astype(BF16), w_branch=w_branch[l].astype(BF16), w_out=w_out[l].astype(BF16),
                  g_mix=norm_mix[l], g_ffn=norm_ffn[l], w_fg=w_ffn_gate[l].astype(BF16),
                  w_fu=w_ffn_up[l].astype(BF16), w_fd=w_ffn_down[l].astype(BF16))
        xp, rows = _prompt_layer(xp, n_p, t_p, lw, topk_p)
        prompt_rows.append(rows)
        xs, rows = _sample_layer(xs, n_dec, t_dec, lw, topk_s, l, page_table, caches, state_d_conv[l], state_d_h[l])
        sample_rows.append(rows)

    y_prompt = rmsnorm(xp, norm_final).reshape(n_p, t_p, D_MODEL)[:, N_META:]
    y_sample = rmsnorm(xs, norm_final).reshape(n_dec, t_dec, D_MODEL)
    stack = lambda rows: tuple(jnp.stack(list(r)) for r in zip(*rows))
    return (y_prompt, y_sample) + stack(prompt_rows) + stack(sample_rows)
```

```python
import functools

import jax
import jax.numpy as jnp
from jax import lax
from jax.experimental import pallas as pl
from jax.experimental.pallas import tpu as pltpu

F32 = jnp.float32
BF16 = jnp.bfloat16
I32 = jnp.int32

D_MODEL = 2048
N_META = 16
HEAD_DIM = 128
N_HEADS = 4
MIX_W = 512
IDX_HEADS = 8
IDX_DIM = 64
TOPK_MAX = 256
RG_C = 8.0
CONV_W = 4
RMS_EPS = 1e-6
PAGE = 128

LANES = 128
SUBLANES = 8
CHUNK = 128
BAND_TILES = 2
VMEM_LIMIT = 56 * 1024 * 1024

OFF_QA, OFF_KA, OFF_VA = 0, 512, 1024
OFF_QB, OFF_KB, OFF_VB = 1536, 2048, 2560
OFF_QC, OFF_QI, OFF_XD = 3072, 3584, 4096
OFF_KC, OFF_VC, OFF_MISC = 4608, 4736, 4864
MISC_KI, MISC_FA, MISC_WI = 0, 64, 68
N_PROJ = 5120

NEG_INF = float("-inf")
INT_MIN = -2 ** 31


def _cparams(*sem):
    return pltpu.CompilerParams(dimension_semantics=sem, vmem_limit_bytes=VMEM_LIMIT)


def _split_bf16(x, pieces):
    out, rest = [], x
    for n in range(pieces):
        part = rest.astype(BF16)
        out.append(part)
        if n + 1 < pieces:
            rest = rest - part.astype(F32)
    return out


def _dot01_left(m01, x, pieces=3):
    return sum(jnp.dot(m01, p, preferred_element_type=F32) for p in _split_bf16(x, pieces))


def _dot01_right(x, m01, pieces=3):
    return sum(jnp.dot(p, m01, preferred_element_type=F32) for p in _split_bf16(x, pieces))


def _dot_nt(a, b):
    return lax.dot_general(a, b, (((1,), (1,)), ((), ())), preferred_element_type=F32)


def _softplus(x, log1p=jnp.log1p):
    return jnp.maximum(x, 0.0) + log1p(jnp.exp(-jnp.abs(x)))


def _softplus_abs(x):
    return _softplus(x, lambda e: jnp.log(1.0 + e))


SB_SPLIT = 2


def _iota(shape, dim):
    return lax.broadcasted_iota(I32, shape, dim)


def _row_tile(m, pref):
    for t in range(min(pref, m), 7, -1):
        if m % t == 0 and t % SUBLANES == 0:
            return t
    raise ValueError(f"no row tile for {m}")


PROJ_TN = 512
ROW_TILES = tuple(off // PROJ_TN for off in (OFF_KA, OFF_VA, OFF_KB, OFF_VB))
TAIL_TILE0 = OFF_XD // PROJ_TN
N_TAIL = N_PROJ - OFF_XD


def _norm_proj_kernel(x_ref, g_ref, wt_ref, tail_ref, o16_ref, h16_ref, ka_ref, va_ref, kb_ref, vb_ref, h_scr):
    j = pl.program_id(1)
    tm = x_ref.shape[0]

    @pl.when(j == 0)
    def _():
        x = x_ref[...]
        y = x * lax.rsqrt(jnp.mean(x * x, axis=-1, keepdims=True) + RMS_EPS)
        h = (y * g_ref[...]).astype(BF16)
        h_scr[...] = h
        h16_ref[...] = h

    acc = _dot_nt(h_scr[...], wt_ref[...])
    o16_ref[...] = acc.astype(BF16)

    @pl.when(j >= TAIL_TILE0)
    def _():
        tail_ref[...] = acc

    for tile, rows_ref in zip(ROW_TILES, (ka_ref, va_ref, kb_ref, vb_ref)):
        @pl.when(j == tile)
        def _(rows_ref=rows_ref):
            for h in range(N_HEADS):
                rows_ref[pl.ds(h, tm, stride=N_HEADS), :] = acc[:, h * HEAD_DIM:(h + 1) * HEAD_DIM]


def norm_proj(x, g, wt16, tm_pref=1032):
    m, d = x.shape
    n = wt16.shape[0]
    tm = _row_tile(m, tm_pref)
    rows_spec = pl.BlockSpec((tm * N_HEADS, HEAD_DIM), lambda i, j: (i, 0))
    rows_shape = jax.ShapeDtypeStruct((m * N_HEADS, HEAD_DIM), F32)
    return pl.pallas_call(
        _norm_proj_kernel,
        grid=(m // tm, n // PROJ_TN),
        in_specs=[pl.BlockSpec((tm, d), lambda i, j: (i, 0), pipeline_mode=pl.Buffered(1)),
                  pl.BlockSpec((1, d), lambda i, j: (0, 0)),
                  pl.BlockSpec((PROJ_TN, d), lambda i, j: (j, 0))],
        out_specs=[pl.BlockSpec((tm, PROJ_TN), lambda i, j: (i, jnp.maximum(j - TAIL_TILE0, 0))),
                   pl.BlockSpec((tm, PROJ_TN), lambda i, j: (i, j)),
                   pl.BlockSpec((tm, d), lambda i, j: (i, 0)),
                   rows_spec, rows_spec, rows_spec, rows_spec],
        out_shape=[jax.ShapeDtypeStruct((m, N_TAIL), F32),
                   jax.ShapeDtypeStruct((m, n), BF16),
                   jax.ShapeDtypeStruct((m, d), BF16),
                   rows_shape, rows_shape, rows_shape, rows_shape],
        scratch_shapes=[pltpu.VMEM((tm, d), BF16)],
        compiler_params=_cparams("parallel", "arbitrary"),
        name="norm_proj",
    )(x, g.reshape(1, d), wt16)


def _logf_kernel(misc_ref, bias_ref, logf_ref, fcum_ref, pad_scr, *, t, n_chunks):
    lane = _iota((1, LANES), 1)
    live = (lane >= MISC_FA) & (lane < MISC_FA + N_HEADS)
    logf = jnp.where(live, jax.nn.log_sigmoid(misc_ref[0] + bias_ref[...]), 0.0)
    logf_ref[0] = logf
    pad_scr[pl.ds(0, t), :] = logf
    if n_chunks * CHUNK > t:
        pad_scr[pl.ds(t, n_chunks * CHUNK - t), :] = jnp.zeros((n_chunks * CHUNK - t, LANES), F32)
    tril = (_iota((CHUNK, CHUNK), 1) <= _iota((CHUNK, CHUNK), 0)).astype(BF16)
    carry = jnp.zeros((1, LANES), F32)
    for c in range(n_chunks):
        f = _dot01_left(tril, pad_scr[pl.ds(c * CHUNK, CHUNK), :]) + carry
        rows = min(CHUNK, t - c * CHUNK)
        fcum_ref[0, pl.ds(c * CHUNK, rows), :] = f[:rows]
        carry = f[CHUNK - 1:CHUNK, :]


def logf_cumsum(misc, bias_row):
    b, t, _ = misc.shape
    n_chunks = pl.cdiv(t, CHUNK)
    return pl.pallas_call(
        functools.partial(_logf_kernel, t=t, n_chunks=n_chunks),
        grid=(b,),
        in_specs=[pl.BlockSpec((1, t, LANES), lambda i: (i, 0, 0)),
                  pl.BlockSpec((1, LANES), lambda i: (0, 0))],
        out_specs=[pl.BlockSpec((1, t, LANES), lambda i: (i, 0, 0)),
                   pl.BlockSpec((1, t, LANES), lambda i: (i, 0, 0))],
        out_shape=[jax.ShapeDtypeStruct((b, t, LANES), F32)] * 2,
        scratch_shapes=[pltpu.VMEM((n_chunks * CHUNK, LANES), F32)],
        compiler_params=_cparams("parallel"),
        name="logf_cumsum",
    )(misc, bias_row)


def _pad_copy(dst, src, t):
    rows = dst.shape[0]
    dst[pl.ds(0, t), :] = src
    if rows > t:
        dst[pl.ds(t, rows - t), :] = jnp.zeros((rows - t, dst.shape[1]), dst.dtype)


def _stage_heads(dst, src_ref, t):
    for h in range(N_HEADS):
        _pad_copy(dst.at[h], src_ref[0, :, h * HEAD_DIM:(h + 1) * HEAD_DIM], t)


def _unstage_heads(o_ref, src, t):
    for h in range(N_HEADS):
        o_ref[0, :, h * HEAD_DIM:(h + 1) * HEAD_DIM] = src[h, 0:t, :]


def _bands(n_tiles):
    return [(lo, min(lo + BAND_TILES, n_tiles)) for lo in range(0, n_tiles, BAND_TILES)]


def _for_heads_and_tiles(n_tiles, tile):
    for lo, hi in _bands(n_tiles):
        def per_head(h, _, lo=lo, hi=hi):
            def per_tile(i, _):
                tile(h, i, hi * CHUNK)
                return 0
            return lax.fori_loop(lo, hi, per_tile, 0)
        lax.fori_loop(0, N_HEADS, per_head, 0)


def _fox_prompt_kernel(q_ref, k_ref, v_ref, frow_ref, o_ref, q_scr, k_scr, v_scr, o_scr, *, t):
    scale = HEAD_DIM ** -0.5
    _stage_heads(q_scr, q_ref, t)
    _stage_heads(k_scr, k_ref, t)
    _stage_heads(v_scr, v_ref, t)

    def tile(h, i, width):
        row0 = pl.multiple_of(i * CHUNK, CHUNK)
        q = q_scr[h, pl.ds(row0, CHUNK), :]
        s = _dot_nt(q, k_scr[h, 0:width, :]) * scale - frow_ref[0, h, :, 0:width]
        vis = _iota((CHUNK, width), 1) <= row0 + _iota((CHUNK, width), 0)
        s = jnp.where(vis, s, NEG_INF)
        p = jnp.exp(s - jnp.max(s, axis=1, keepdims=True))
        l = jnp.sum(p, axis=1, keepdims=True)
        o = jnp.dot(p.astype(BF16), v_scr[h, 0:width, :], preferred_element_type=F32) / l
        o_scr[h, pl.ds(row0, CHUNK), :] = o.astype(o_scr.dtype)

    _for_heads_and_tiles(k_scr.shape[1] // CHUNK, tile)
    _unstage_heads(o_ref, o_scr, t)


def _head_scratch(tp, n):
    return [pltpu.VMEM((N_HEADS, tp, HEAD_DIM), BF16)] * n


def fox_prompt(proj16, frow):
    b, t, _ = proj16.shape
    tp = frow.shape[-1]
    blk = lambda off: pl.BlockSpec((1, t, MIX_W), lambda i, o=off // MIX_W: (i, 0, o))
    return pl.pallas_call(
        functools.partial(_fox_prompt_kernel, t=t),
        grid=(b,),
        in_specs=[blk(OFF_QA), blk(OFF_KA), blk(OFF_VA),
                  pl.BlockSpec((1, N_HEADS, 1, tp), lambda i: (i, 0, 0, 0))],
        out_specs=pl.BlockSpec((1, t, MIX_W), lambda i: (i, 0, 0)),
        out_shape=jax.ShapeDtypeStruct((b, t, MIX_W), BF16),
        scratch_shapes=_head_scratch(tp, 4),
        compiler_params=_cparams("parallel"),
        name="fox_prompt",
    )(proj16, proj16, proj16, frow)


def _suffix_matrix():
    return (_iota((CHUNK, CHUNK), 0) > _iota((CHUNK, CHUNK), 1)).astype(BF16)


def _sb_prompt_kernel(q_ref, k_ref, v_ref, o_ref, q_scr, k_scr, v_scr, o_scr, *, t):
    scale = HEAD_DIM ** -0.5
    _stage_heads(q_scr, q_ref, t)
    _stage_heads(k_scr, k_ref, t)
    _stage_heads(v_scr, v_ref, t)
    suffix = _suffix_matrix()

    def tile(h, i, width):
        row0 = pl.multiple_of(i * CHUNK, CHUNK)
        q = q_scr[h, pl.ds(row0, CHUNK), :]
        z = _dot_nt(q, k_scr[h, 0:width, :]) * scale
        vis = _iota((CHUNK, width), 1) < row0 + _iota((CHUNK, width), 0)
        log_keep = jnp.where(vis, -_softplus_abs(z), 0.0)
        later_chunks = jnp.zeros((CHUNK, 1), F32)
        pieces = [None] * (width // CHUNK)
        for c in reversed(range(width // CHUNK)):
            cs = slice(c * CHUNK, (c + 1) * CHUNK)
            later = _dot01_right(log_keep[:, cs], suffix, SB_SPLIT) + later_chunks
            att = jnp.where(vis[:, cs], jnp.exp(log_keep[:, cs] + z[:, cs] + later), 0.0)
            pieces[c] = att.astype(BF16)
            later_chunks = later_chunks + jnp.sum(log_keep[:, cs], axis=1, keepdims=True)
        att = jnp.concatenate(pieces, axis=1)
        o = jnp.dot(att, v_scr[h, 0:width, :], preferred_element_type=F32)
        o_scr[h, pl.ds(row0, CHUNK), :] = o.astype(o_scr.dtype)

    _for_heads_and_tiles(k_scr.shape[1] // CHUNK, tile)
    _unstage_heads(o_ref, o_scr, t)


def sb_prompt(proj16):
    b, t, _ = proj16.shape
    tp = pl.cdiv(t, CHUNK) * CHUNK
    blk = lambda off: pl.BlockSpec((1, t, MIX_W), lambda i, o=off // MIX_W: (i, 0, o))
    return pl.pallas_call(
        functools.partial(_sb_prompt_kernel, t=t),
        grid=(b,),
        in_specs=[blk(OFF_QB), blk(OFF_KB), blk(OFF_VB)],
        out_specs=pl.BlockSpec((1, t, MIX_W), lambda i: (i, 0, 0)),
        out_shape=jax.ShapeDtypeStruct((b, t, MIX_W), BF16),
        scratch_shapes=_head_scratch(tp, 4),
        compiler_params=_cparams("parallel"),
        name="sb_prompt",
    )(proj16, proj16, proj16)


def _order_key(score):
    score = jnp.where(score == 0.0, 0.0, score)
    bits = lax.bitcast_convert_type(score, I32)
    return bits ^ ((bits >> 31) & 0x7FFFFFFF)


def _select_topk(key_ref, bias_ref, vis_fn, rows, width, topk):
    kf = float(topk)

    def count(pred):
        return jnp.sum(jnp.where(pred(key_ref[0:rows, 0:width]), 1.0, 0.0), axis=1, keepdims=True)

    thr0 = jnp.where(count(lambda k: k >= 0) >= kf, 0, INT_MIN).astype(I32)

    def bit_step(b, thr):
        cand = thr | (jnp.int32(1) << (30 - b))
        return jnp.where(count(lambda k: k >= cand) >= kf, cand, thr)

    thr = lax.fori_loop(0, 31, bit_step, thr0)
    need = kf - count(lambda k: k > thr)
    keys = key_ref[0:rows, 0:width]
    vis = vis_fn(0, width)
    n_tie = jnp.sum(jnp.where((keys == thr) & vis, 1.0, 0.0), axis=1, keepdims=True)
    bias_ref[0:rows, 0:width] = jnp.where((keys >= thr) & vis, 0.0, NEG_INF)

    @pl.when(jnp.max(n_tie - need) > 0.0)
    def _():
        prefix = (_iota((CHUNK, CHUNK), 0) <= _iota((CHUNK, CHUNK), 1)).astype(BF16)
        seen = jnp.zeros((rows, 1), F32)
        for c in range(width // CHUNK):
            k = key_ref[0:rows, c * CHUNK:(c + 1) * CHUNK]
            v = vis_fn(c * CHUNK, CHUNK)
            tie = jnp.where((k == thr) & v, 1.0, 0.0)
            rank = jnp.dot(tie.astype(BF16), prefix, preferred_element_type=F32) + seen
            sel = ((k > thr) & v) | ((tie > 0.0) & (rank <= need))
            bias_ref[0:rows, c * CHUNK:(c + 1) * CHUNK] = jnp.where(sel, 0.0, NEG_INF)
            seen = seen + jnp.sum(tie, axis=1, keepdims=True)


def _stack_heads(q):
    return jnp.concatenate([q[:, h * HEAD_DIM:(h + 1) * HEAD_DIM] for h in range(N_HEADS)], axis=0)


def _dsa_prompt_kernel(q_ref, qi_ref, kc_ref, vc_ref, misc32_ref, misc16_ref, o_ref,
                       kc_scr, vc_scr, ki_scr, key_scr, bias_scr, *, t, topk):
    n_full, tail = t // CHUNK, t % CHUNK
    scale = HEAD_DIM ** -0.5
    idx_scale = (IDX_DIM * IDX_HEADS) ** -0.5
    _pad_copy(kc_scr, kc_ref[0], t)
    _pad_copy(vc_scr, vc_ref[0], t)
    _pad_copy(ki_scr, misc16_ref[0], t)

    def tile(row0, tq, width):
        def vis_fn(c0, w):
            return (c0 + _iota((tq, w), 1)) <= (row0 + _iota((tq, w), 0))

        qi = qi_ref[0, pl.ds(row0, tq), :]
        wi = misc32_ref[0, pl.ds(row0, tq), MISC_WI:MISC_WI + IDX_HEADS]
        ki = ki_scr[0:width, MISC_KI:MISC_KI + IDX_DIM]
        score = None
        for h in range(IDX_HEADS):
            term = wi[:, h:h + 1] * jnp.maximum(_dot_nt(qi[:, h * IDX_DIM:(h + 1) * IDX_DIM], ki), 0.0)
            score = term if score is None else score + term
        key_scr[0:tq, 0:width] = jnp.where(vis_fn(0, width), _order_key(score * idx_scale), INT_MIN)
        _select_topk(key_scr, bias_scr, vis_fn, tq, width, topk)

        q4 = _stack_heads(q_ref[0, pl.ds(row0, tq), :])
        bias = bias_scr[0:tq, 0:width]
        s = _dot_nt(q4, kc_scr[0:width, :]) * scale + jnp.concatenate([bias] * N_HEADS, axis=0)
        p = jnp.exp(s - jnp.max(s, axis=1, keepdims=True))
        l = jnp.sum(p, axis=1, keepdims=True)
        out = jnp.dot(p.astype(BF16), vc_scr[0:width, :], preferred_element_type=F32) / l
        for h in range(N_HEADS):
            o_ref[0, pl.ds(row0, tq), h * HEAD_DIM:(h + 1) * HEAD_DIM] = out[h * tq:(h + 1) * tq].astype(o_ref.dtype)

    for lo, hi in _bands(n_full):
        tile(lo * CHUNK, (hi - lo) * CHUNK, hi * CHUNK)
    if tail:
        tile(n_full * CHUNK, tail, (n_full + 1) * CHUNK)


def dsa_prompt(tail32, proj16, topk):
    b, t, _ = proj16.shape
    tp = pl.cdiv(t, CHUNK) * CHUNK
    wide = lambda off: pl.BlockSpec((1, t, MIX_W), lambda i, o=off // MIX_W: (i, 0, o))
    narrow = lambda off: pl.BlockSpec((1, t, LANES), lambda i, o=off // LANES: (i, 0, o))
    return pl.pallas_call(
        functools.partial(_dsa_prompt_kernel, t=t, topk=topk),
        grid=(b,),
        in_specs=[wide(OFF_QC), wide(OFF_QI), narrow(OFF_KC), narrow(OFF_VC), narrow(OFF_MISC - OFF_XD), narrow(OFF_MISC)],
        out_specs=pl.BlockSpec((1, t, MIX_W), lambda i: (i, 0, 0)),
        out_shape=jax.ShapeDtypeStruct((b, t, MIX_W), BF16),
        scratch_shapes=[pltpu.VMEM((tp, HEAD_DIM), BF16), pltpu.VMEM((tp, HEAD_DIM), BF16),
                        pltpu.VMEM((tp, LANES), BF16),
                        pltpu.VMEM((BAND_TILES * CHUNK, tp), I32), pltpu.VMEM((BAND_TILES * CHUNK, tp), F32)],
        compiler_params=_cparams("parallel"),
        name="dsa_prompt",
    )(proj16, proj16, proj16, proj16, tail32, proj16)


CONV_PAD = 8


def _rglru_kernel(xd_ref, cs_ref, h0_ref, cw_ref, cb_ref, wa_ref, ba_ref, wx_ref, bx_ref, lam_ref,
                  y_ref, hl_ref, xpad_scr, a_scr, u_scr, hs_scr, *, t):
    n_full, tail = t // CHUNK, t % CHUNK
    xpad_scr[pl.ds(0, CONV_PAD), :] = jnp.zeros((CONV_PAD, MIX_W), F32)
    xpad_scr[pl.ds(CONV_PAD - (CONV_W - 1), CONV_W - 1), :] = cs_ref[0]
    xpad_scr[pl.ds(CONV_PAD, t), :] = xd_ref[0]
    decay_rate = -RG_C * _softplus(-lam_ref[...])

    def chunk(r0, rows, h):
        win = xpad_scr[pl.ds(r0, rows + CONV_PAD), :]
        xc = cb_ref[...]
        for i in range(CONV_W):
            lo = CONV_PAD - (CONV_W - 1) + i
            xc = xc + win[lo:lo + rows] * cw_ref[i:i + 1, :]
        xc16 = xc.astype(BF16)
        gate_r = jax.nn.sigmoid(jnp.dot(xc16, wa_ref[...], preferred_element_type=F32) + ba_ref[...])
        gate_i = jax.nn.sigmoid(jnp.dot(xc16, wx_ref[...], preferred_element_type=F32) + bx_ref[...])
        log_a = gate_r * decay_rate
        a_scr[0:rows, :] = jnp.exp(log_a)
        th = jnp.tanh(log_a)
        u_scr[0:rows, :] = jnp.sqrt(-2.0 * th / (1.0 - th)) * gate_i * xc

        def step(r, hh):
            hh = a_scr[pl.ds(r, 1), :] * hh + u_scr[pl.ds(r, 1), :]
            hs_scr[pl.ds(r, 1), :] = hh
            return hh
        h = lax.fori_loop(0, rows, step, h, unroll=8)
        y_ref[0, pl.ds(r0, rows), :] = hs_scr[0:rows, :].astype(y_ref.dtype)
        return h

    h = h0_ref[0]
    if n_full:
        h = lax.fori_loop(0, n_full, lambda i, hh: chunk(pl.multiple_of(i * CHUNK, CHUNK), CHUNK, hh), h)
    if tail:
        h = chunk(n_full * CHUNK, tail, h)
    hl_ref[0] = h


def rglru(tail32, conv_state, h0, lw, out_dtype):
    b, t, _ = tail32.shape
    conv_w, conv_b, wa16, b_a, wx16, b_x, lam = (lw[k] for k in ("conv_w", "conv_b", "w_a", "b_a", "w_x", "b_x", "lam"))
    row = lambda v: v.reshape(1, MIX_W)
    const = lambda shape: pl.BlockSpec(shape, lambda i: (0,) * len(shape))
    return pl.pallas_call(
        functools.partial(_rglru_kernel, t=t),
        grid=(b,),
        in_specs=[pl.BlockSpec((1, t, MIX_W), lambda i: (i, 0, 0)),
                  pl.BlockSpec((1, CONV_W - 1, MIX_W), lambda i: (i, 0, 0)),
                  pl.BlockSpec((1, 1, MIX_W), lambda i: (i, 0, 0)),
                  const((CONV_W, MIX_W)), const((1, MIX_W)),
                  const((MIX_W, MIX_W)), const((1, MIX_W)),
                  const((MIX_W, MIX_W)), const((1, MIX_W)), const((1, MIX_W))],
        out_specs=[pl.BlockSpec((1, t, MIX_W), lambda i: (i, 0, 0)),
                   pl.BlockSpec((1, 1, MIX_W), lambda i: (i, 0, 0))],
        out_shape=[jax.ShapeDtypeStruct((b, t, MIX_W), out_dtype), jax.ShapeDtypeStruct((b, 1, MIX_W), F32)],
        scratch_shapes=[pltpu.VMEM((CONV_PAD + pl.cdiv(t, CHUNK) * CHUNK + CONV_PAD, MIX_W), F32),
                        pltpu.VMEM((CHUNK, MIX_W), F32), pltpu.VMEM((CHUNK, MIX_W), F32),
                        pltpu.VMEM((CHUNK, MIX_W), F32)],
        compiler_params=_cparams("parallel"),
        name="rglru",
    )(tail32, conv_state, h0.reshape(b, 1, MIX_W), conv_w, row(conv_b), wa16, row(b_a), wx16, row(b_x), row(lam))


def _block_diag(w):
    n, c, e = w.shape
    eye = jnp.eye(n, dtype=w.dtype)
    return (eye[:, None, :, None] * w[:, :, None, :]).reshape(n * c, n * e)


def _merge_kernel(h_ref, ya_ref, yb_ref, yc_ref, yd_ref, g0_ref, g1_ref, g2_ref, g3_ref, wb_ref, o_ref):
    h = h_ref[...]
    acc = None
    for n, (y_ref, g_ref) in enumerate(((ya_ref, g0_ref), (yb_ref, g1_ref), (yc_ref, g2_ref), (yd_ref, g3_ref))):
        gate = jax.nn.sigmoid(jnp.dot(h, g_ref[...], preferred_element_type=F32))
        term = gate * jnp.dot(y_ref[...], wb_ref[n], preferred_element_type=F32)
        acc = term if acc is None else acc + term
    o_ref[...] = acc.astype(o_ref.dtype)


def merge(h16, ys, wg16, wb16, tm_pref=512, tn=512):
    m, d = h16.shape
    tm = _row_tile(m, tm_pref)
    nj = d // tn
    y_spec = pl.BlockSpec((tm, MIX_W), lambda i, j: (i, 0))
    g_spec = lambda n: pl.BlockSpec((d, tn), lambda i, j, n=n: (0, n * nj + j))
    return pl.pallas_call(
        _merge_kernel,
        grid=(m // tm, nj),
        in_specs=[pl.BlockSpec((tm, d), lambda i, j: (i, 0)), y_spec, y_spec, y_spec, y_spec,
                  g_spec(0), g_spec(1), g_spec(2), g_spec(3),
                  pl.BlockSpec((4, MIX_W, tn), lambda i, j: (0, 0, j))],
        out_specs=pl.BlockSpec((tm, tn), lambda i, j: (i, j)),
        out_shape=jax.ShapeDtypeStruct((m, d), BF16),
        compiler_params=_cparams("parallel", "arbitrary"),
        name="merge",
    )(h16, *ys, wg16, wg16, wg16, wg16, wb16)


def _matmul_res_kernel(a_ref, w_ref, x_ref, o_ref):
    o_ref[...] = x_ref[...] + jnp.dot(a_ref[...], w_ref[...], preferred_element_type=F32)


def matmul_residual(a16, w16, x, tm_pref=344):
    m, k = a16.shape
    n = w16.shape[1]
    tm = _row_tile(m, tm_pref)
    return pl.pallas_call(
        _matmul_res_kernel,
        grid=(m // tm,),
        in_specs=[pl.BlockSpec((tm, k), lambda i: (i, 0)),
                  pl.BlockSpec((k, n), lambda i: (0, 0)),
                  pl.BlockSpec((tm, n), lambda i: (i, 0))],
        out_specs=pl.BlockSpec((tm, n), lambda i: (i, 0)),
        out_shape=jax.ShapeDtypeStruct((m, n), F32),
        compiler_params=_cparams("parallel"),
        name="out_proj",
    )(a16, w16, x)


def _ffn_kernel(x_ref, g_ref, wg_ref, wu_ref, wd_ref, o_ref, h_scr, acc_scr):
    f = pl.program_id(1)

    @pl.when(f == 0)
    def _():
        x = x_ref[...]
        y = x * lax.rsqrt(jnp.mean(x * x, axis=-1, keepdims=True) + RMS_EPS)
        h_scr[...] = (y * g_ref[...]).astype(BF16)
        acc_scr[...] = jnp.zeros_like(acc_scr)

    h = h_scr[...]
    gate = jnp.dot(h, wg_ref[...], preferred_element_type=F32)
    up = jnp.dot(h, wu_ref[...], preferred_element_type=F32)
    act = (jax.nn.silu(gate) * up).astype(BF16)
    acc_scr[...] += jnp.dot(act, wd_ref[...], preferred_element_type=F32)

    @pl.when(f == pl.num_programs(1) - 1)
    def _():
        o_ref[...] = x_ref[...] + acc_scr[...]


def ffn(x, g, wg16, wu16, wd16, tm_pref=688, tf=512):
    m, d = x.shape
    dff = wg16.shape[1]
    tm = _row_tile(m, tm_pref)
    return pl.pallas_call(
        _ffn_kernel,
        grid=(m // tm, dff // tf),
        in_specs=[pl.BlockSpec((tm, d), lambda i, f: (i, 0)),
                  pl.BlockSpec((1, d), lambda i, f: (0, 0)),
                  pl.BlockSpec((d, tf), lambda i, f: (0, f)),
                  pl.BlockSpec((d, tf), lambda i, f: (0, f)),
                  pl.BlockSpec((tf, d), lambda i, f: (f, 0))],
        out_specs=pl.BlockSpec((tm, d), lambda i, f: (i, 0)),
        out_shape=jax.ShapeDtypeStruct((m, d), F32),
        scratch_shapes=[pltpu.VMEM((tm, d), BF16), pltpu.VMEM((tm, d), F32)],
        compiler_params=_cparams("parallel", "arbitrary"),
        name="ffn",
    )(x, g.reshape(1, d), wg16, wu16, wd16)


def _rmsnorm_kernel(x_ref, g_ref, o_ref):
    x = x_ref[...]
    o_ref[...] = x * lax.rsqrt(jnp.mean(x * x, axis=-1, keepdims=True) + RMS_EPS) * g_ref[...]


def rmsnorm(x, g, tm_pref=512):
    m, d = x.shape
    tm = _row_tile(m, tm_pref)
    return pl.pallas_call(
        _rmsnorm_kernel,
        grid=(m // tm,),
        in_specs=[pl.BlockSpec((tm, d), lambda i: (i, 0)), pl.BlockSpec((1, d), lambda i: (0, 0))],
        out_specs=pl.BlockSpec((tm, d), lambda i: (i, 0)),
        out_shape=jax.ShapeDtypeStruct((m, d), F32),
        compiler_params=_cparams("parallel"),
        name="final_norm",
    )(x, g.reshape(1, d))


PAGES_PER_STEP = 16
STEP_KEYS = PAGES_PER_STEP * PAGE


def _page_specs(block, layer, n_pages, reverse):
    def spec(p):
        def index(b, s, pt):
            j = s * PAGES_PER_STEP + p
            if reverse:
                j = n_pages - 1 - j
            return (layer, pt[b, j], 0, 0)
        return pl.BlockSpec((1, 1) + block, index)
    return [spec(p) for p in range(PAGES_PER_STEP)]


def _per_seq(shape):
    return pl.BlockSpec((1,) + shape, lambda b, s, pt: (b,) + (0,) * len(shape))


def _head_page(ref, h):
    return ref.at[0, 0][pl.ds(h, PAGE, stride=N_HEADS), :].astype(BF16)


def _own_head_rows(per_head, t):
    return jnp.concatenate([per_head[h][h * t:(h + 1) * t] for h in range(N_HEADS)], axis=0)


def _query_index(t, width):
    return jnp.concatenate([_iota((t, width), 0)] * N_HEADS, axis=0)


def _pad_rows(dst, src):
    dst[...] = jnp.zeros(dst.shape, dst.dtype)
    dst[pl.ds(0, src.shape[0]), :] = src


def _per_head_rows(x, t):
    return jnp.concatenate([jnp.broadcast_to(x[h:h + 1], (t, x.shape[1])) for h in range(N_HEADS)], axis=0)


def _per_head_cols(x):
    return jnp.concatenate([x[:, h:h + 1] for h in range(N_HEADS)], axis=0)


def _write_stacked(o_ref, out, t):
    for h in range(N_HEADS):
        o_ref[0, :, h * HEAD_DIM:(h + 1) * HEAD_DIM] = out[h * t:(h + 1) * t].astype(o_ref.dtype)


def _softmax_update(carry, s, pv_fn):
    m, l, acc = carry
    m_new = jnp.maximum(m, jnp.max(s, axis=1, keepdims=True))
    m_safe = jnp.where(m_new == NEG_INF, 0.0, m_new)
    alpha = jnp.exp(m - m_safe)
    p = jnp.exp(s - m_safe)
    l = alpha * l + jnp.sum(p, axis=1, keepdims=True)
    return m_new, l, alpha * acc + pv_fn(p.astype(BF16))


def _softmax_init(rows):
    return (jnp.full((rows, 1), NEG_INF, F32), jnp.zeros((rows, 1), F32), jnp.zeros((rows, HEAD_DIM), F32))


def _new_keys_per_head(q4, pad_ref, t):
    return _own_head_rows([_dot_nt(q4, pad_ref[:, h * HEAD_DIM:(h + 1) * HEAD_DIM].astype(BF16))
                           for h in range(N_HEADS)], t)


def _new_values_per_head(p16, pad_ref, t):
    return _own_head_rows([jnp.dot(p16, pad_ref[:, h * HEAD_DIM:(h + 1) * HEAD_DIM].astype(BF16),
                                   preferred_element_type=F32) for h in range(N_HEADS)], t)


def _page_scores(q4, k_refs, t):
    return _own_head_rows([_dot_nt(q4, jnp.concatenate([_head_page(k_ref, h) for k_ref in k_refs], axis=0))
                           for h in range(N_HEADS)], t)


def _page_values(p16, v_refs, t):
    return _own_head_rows([jnp.dot(p16, jnp.concatenate([_head_page(v_ref, h) for v_ref in v_refs], axis=0),
                                   preferred_element_type=F32) for h in range(N_HEADS)], t)


def _fox_sample_kernel(pt_ref, q_ref, kn_ref, vn_ref, cn_ref, cnrow_ref, *rest, t):
    n = PAGES_PER_STEP
    k_refs, v_refs, lf_refs = rest[0:n], rest[n:2 * n], rest[2 * n:3 * n]
    o_ref, kpad, vpad, m_scr, l_scr, acc_scr, d_scr = rest[3 * n:]
    s_id = pl.program_id(1)
    scale = HEAD_DIM ** -0.5
    rows = N_HEADS * t
    q4 = _stack_heads(q_ref[0]).astype(BF16)
    cn = _per_head_cols(cn_ref[0])

    @pl.when(s_id == 0)
    def _():
        _pad_rows(kpad, kn_ref[0])
        _pad_rows(vpad, vn_ref[0])
        s = _new_keys_per_head(q4, kpad, t) * scale + cn - _per_head_rows(cnrow_ref[0], t)
        col = _iota((rows, CHUNK), 1)
        s = jnp.where((col <= _query_index(t, CHUNK)) & (col < t), s, NEG_INF)
        m, l, acc = _softmax_update(_softmax_init(rows), s, lambda p16: _new_values_per_head(p16, vpad, t))
        m_scr[...], l_scr[...], acc_scr[...] = m, l, acc
        d_scr[...] = jnp.zeros(d_scr.shape, F32)

    suffix = _suffix_matrix()
    later = d_scr[...]
    decays = []
    for p in range(n):
        lf = jnp.concatenate([lf_refs[p][0, 0], jnp.zeros((SUBLANES - N_HEADS, PAGE), F32)], axis=0)
        decays.append(_dot01_right(lf, suffix) + later)
        later = later + jnp.sum(lf, axis=1, keepdims=True)
    decay = jnp.concatenate(decays, axis=1)
    s = _page_scores(q4, k_refs, t) * scale + _per_head_rows(decay[0:N_HEADS], t) + cn
    carry = _softmax_update((m_scr[...], l_scr[...], acc_scr[...]), s, lambda p16: _page_values(p16, v_refs, t))
    m_scr[...], l_scr[...], acc_scr[...] = carry
    d_scr[...] = later

    @pl.when(s_id == pl.num_programs(1) - 1)
    def _():
        _write_stacked(o_ref, carry[2] / carry[1], t)


def _kv_page_block():
    return (PAGE * N_HEADS, HEAD_DIM)


def fox_sample(page_table, layer, q, kn, vn, cn, cnrow, cache_k, cache_v, cache_lft):
    b, t, _ = q.shape
    n_pages = page_table.shape[1]
    rows = N_HEADS * t
    grid_spec = pltpu.PrefetchScalarGridSpec(
        num_scalar_prefetch=1,
        grid=(b, n_pages // PAGES_PER_STEP),
        in_specs=[_per_seq((t, MIX_W)), _per_seq((t, MIX_W)), _per_seq((t, MIX_W)),
                  _per_seq((t, N_HEADS)), _per_seq((N_HEADS, CHUNK))]
                 + _page_specs(_kv_page_block(), layer, n_pages, True)
                 + _page_specs(_kv_page_block(), layer, n_pages, True)
                 + _page_specs((N_HEADS, PAGE), layer, n_pages, True),
        out_specs=_per_seq((t, MIX_W)),
        scratch_shapes=[pltpu.VMEM((CHUNK, MIX_W), F32), pltpu.VMEM((CHUNK, MIX_W), F32),
                        pltpu.VMEM((rows, 1), F32), pltpu.VMEM((rows, 1), F32), pltpu.VMEM((rows, HEAD_DIM), F32),
                        pltpu.VMEM((SUBLANES, 1), F32)])
    return pl.pallas_call(
        functools.partial(_fox_sample_kernel, t=t),
        grid_spec=grid_spec,
        out_shape=jax.ShapeDtypeStruct((b, t, MIX_W), F32),
        compiler_params=_cparams("parallel", "arbitrary"),
        name="fox_sample",
    )(page_table, q, kn, vn, cn, cnrow, *([cache_k] * PAGES_PER_STEP), *([cache_v] * PAGES_PER_STEP),
      *([cache_lft] * PAGES_PER_STEP))


def _sb_sample_kernel(pt_ref, q_ref, kn_ref, vn_ref, *rest, t):
    n = PAGES_PER_STEP
    k_refs, v_refs = rest[0:n], rest[n:2 * n]
    o_ref, kpad, vpad, r_scr, acc_scr = rest[2 * n:]
    s_id = pl.program_id(1)
    scale = HEAD_DIM ** -0.5
    rows = N_HEADS * t
    q4 = _stack_heads(q_ref[0]).astype(BF16)
    suffix = _suffix_matrix()

    @pl.when(s_id == 0)
    def _():
        _pad_rows(kpad, kn_ref[0])
        _pad_rows(vpad, vn_ref[0])
        z = _new_keys_per_head(q4, kpad, t) * scale
        vis = _iota((rows, CHUNK), 1) < _query_index(t, CHUNK)
        log_keep = jnp.where(vis, -_softplus_abs(z), 0.0)
        att = jnp.where(vis, jnp.exp(log_keep + z + _dot01_right(log_keep, suffix, SB_SPLIT)), 0.0)
        acc_scr[...] = _new_values_per_head(att.astype(BF16), vpad, t)
        r_scr[...] = jnp.sum(log_keep, axis=1, keepdims=True)

    later = r_scr[...]
    z = _page_scores(q4, k_refs, t) * scale
    log_keep = -_softplus_abs(z)
    page = lambda x, p: x[:, p * PAGE:(p + 1) * PAGE]
    in_page = _dot01_right(jnp.concatenate([page(log_keep, p) for p in range(n)], axis=0), suffix, SB_SPLIT)
    pieces = []
    for p in range(n):
        suf = in_page[p * rows:(p + 1) * rows]
        pieces.append(jnp.exp(page(log_keep, p) + page(z, p) + suf + later).astype(BF16))
        later = later + suf[:, 0:1] + page(log_keep, p)[:, 0:1]
    acc = acc_scr[...] + _page_values(jnp.concatenate(pieces, axis=1), v_refs, t)
    r_scr[...], acc_scr[...] = later, acc

    @pl.when(s_id == pl.num_programs(1) - 1)
    def _():
        _write_stacked(o_ref, acc, t)


def sb_sample(page_table, layer, q, kn, vn, cache_k, cache_v):
    b, t, _ = q.shape
    n_pages = page_table.shape[1]
    rows = N_HEADS * t
    grid_spec = pltpu.PrefetchScalarGridSpec(
        num_scalar_prefetch=1,
        grid=(b, n_pages // PAGES_PER_STEP),
        in_specs=[_per_seq((t, MIX_W))] * 3
                 + _page_specs(_kv_page_block(), layer, n_pages, True)
                 + _page_specs(_kv_page_block(), layer, n_pages, True),
        out_specs=_per_seq((t, MIX_W)),
        scratch_shapes=[pltpu.VMEM((CHUNK, MIX_W), F32), pltpu.VMEM((CHUNK, MIX_W), F32),
                        pltpu.VMEM((rows, 1), F32), pltpu.VMEM((rows, HEAD_DIM), F32)])
    return pl.pallas_call(
        functools.partial(_sb_sample_kernel, t=t),
        grid_spec=grid_spec,
        out_shape=jax.ShapeDtypeStruct((b, t, MIX_W), F32),
        compiler_params=_cparams("parallel", "arbitrary"),
        name="sb_sample",
    )(page_table, q, kn, vn, *([cache_k] * PAGES_PER_STEP), *([cache_v] * PAGES_PER_STEP))


def _weighted_relu_sum(g, wi, t):
    score = None
    for h in range(IDX_HEADS):
        term = wi[:, h:h + 1] * jnp.maximum(g[h * t:(h + 1) * t], 0.0)
        score = term if score is None else score + term
    return score * ((IDX_DIM * IDX_HEADS) ** -0.5)


def _dsa_scores_kernel(pt_ref, qi_ref, misc_ref, *rest, t):
    n = PAGES_PER_STEP
    idx_refs = rest[0:n]
    past_ref, new_ref, kpad = rest[n:]
    qi = qi_ref[0]
    qi_stack = jnp.concatenate([qi[:, h * IDX_DIM:(h + 1) * IDX_DIM] for h in range(IDX_HEADS)], axis=0).astype(BF16)
    wi = misc_ref[0, :, MISC_WI:MISC_WI + IDX_HEADS]

    @pl.when(pl.program_id(1) == 0)
    def _():
        _pad_rows(kpad, misc_ref[0])
        g = _dot_nt(qi_stack, kpad[:, MISC_KI:MISC_KI + IDX_DIM].astype(BF16))
        new_ref[0] = _weighted_relu_sum(g, wi, t)

    for p in range(n):
        g = jnp.dot(qi_stack, idx_refs[p][0, 0].astype(BF16), preferred_element_type=F32)
        past_ref[0, :, p * PAGE:(p + 1) * PAGE] = _weighted_relu_sum(g, wi, t)


def dsa_sample_scores(page_table, layer, qi, misc, cache_idx_t):
    b, t, _ = qi.shape
    n_pages = page_table.shape[1]
    grid_spec = pltpu.PrefetchScalarGridSpec(
        num_scalar_prefetch=1,
        grid=(b, n_pages // PAGES_PER_STEP),
        in_specs=[_per_seq((t, MIX_W)), _per_seq((t, LANES))] + _page_specs((IDX_DIM, PAGE), layer, n_pages, False),
        out_specs=[pl.BlockSpec((1, t, STEP_KEYS), lambda i, s, pt: (i, 0, s)), _per_seq((t, CHUNK))],
        scratch_shapes=[pltpu.VMEM((CHUNK, LANES), F32)])
    return pl.pallas_call(
        functools.partial(_dsa_scores_kernel, t=t),
        grid_spec=grid_spec,
        out_shape=[jax.ShapeDtypeStruct((b, t, n_pages * PAGE), F32), jax.ShapeDtypeStruct((b, t, CHUNK), F32)],
        compiler_params=_cparams("parallel", "arbitrary"),
        name="dsa_sample_scores",
    )(page_table, qi, misc, *([cache_idx_t] * PAGES_PER_STEP))


SELECT_ROWS = 64


def _dsa_select_kernel(score_ref, bias_ref, key_scr, *, t, n_past, topk):
    rows, width = score_ref.shape
    qidx = jnp.concatenate([_iota((t, 1), 0)] * (rows // t), axis=0)

    def vis_fn(c0, w):
        col = c0 + _iota((rows, w), 1)
        return (col < n_past) | ((col - n_past <= qidx) & (col - n_past < t))

    key_scr[...] = jnp.where(vis_fn(0, width), _order_key(score_ref[...]), INT_MIN)
    _select_topk(key_scr, bias_ref, vis_fn, rows, width, topk)


def dsa_sample_select(scores, t, n_past, topk):
    m, width = scores.shape
    rows = _row_tile(m, SELECT_ROWS)
    assert rows % t == 0
    spec = pl.BlockSpec((rows, width), lambda i: (i, 0))
    return pl.pallas_call(
        functools.partial(_dsa_select_kernel, t=t, n_past=n_past, topk=topk),
        grid=(m // rows,),
        in_specs=[spec],
        out_specs=spec,
        out_shape=jax.ShapeDtypeStruct(scores.shape, F32),
        scratch_shapes=[pltpu.VMEM((rows, width), I32)],
        compiler_params=_cparams("parallel"),
        name="dsa_sample_select",
    )(scores)


def _dsa_attend_kernel(pt_ref, q_ref, kn_ref, vn_ref, bias_new_ref, bias_past_ref, *rest, t):
    n = PAGES_PER_STEP
    k_refs, v_refs = rest[0:n], rest[n:2 * n]
    o_ref, kpad, vpad, m_scr, l_scr, acc_scr = rest[2 * n:]
    s_id = pl.program_id(1)
    scale = HEAD_DIM ** -0.5
    rows = N_HEADS * t
    q4 = _stack_heads(q_ref[0]).astype(BF16)

    @pl.when(s_id == 0)
    def _():
        _pad_rows(kpad, kn_ref[0])
        _pad_rows(vpad, vn_ref[0])
        s = _dot_nt(q4, kpad[...].astype(BF16)) * scale + jnp.concatenate([bias_new_ref[0]] * N_HEADS, axis=0)
        m, l, acc = _softmax_update(
            _softmax_init(rows), s,
            lambda p16: jnp.dot(p16, vpad[...].astype(BF16), preferred_element_type=F32))
        m_scr[...], l_scr[...], acc_scr[...] = m, l, acc

    step_rows = lambda refs: jnp.concatenate([r[0, 0].astype(BF16) for r in refs], axis=0)
    s = _dot_nt(q4, step_rows(k_refs)) * scale + jnp.concatenate([bias_past_ref[0]] * N_HEADS, axis=0)
    carry = _softmax_update((m_scr[...], l_scr[...], acc_scr[...]), s,
                            lambda p16: jnp.dot(p16, step_rows(v_refs), preferred_element_type=F32))
    m_scr[...], l_scr[...], acc_scr[...] = carry

    @pl.when(s_id == pl.num_programs(1) - 1)
    def _():
        _write_stacked(o_ref, carry[2] / carry[1], t)


def dsa_sample_attend(page_table, layer, q, kn, vn, bias, cache_k, cache_v):
    b, t, _ = q.shape
    n_pages = page_table.shape[1]
    rows = N_HEADS * t
    grid_spec = pltpu.PrefetchScalarGridSpec(
        num_scalar_prefetch=1,
        grid=(b, n_pages // PAGES_PER_STEP),
        in_specs=[_per_seq((t, MIX_W)), _per_seq((t, HEAD_DIM)), _per_seq((t, HEAD_DIM)),
                  pl.BlockSpec((1, t, CHUNK), lambda i, s, pt: (i, 0, n_pages)),
                  pl.BlockSpec((1, t, STEP_KEYS), lambda i, s, pt: (i, 0, s))]
                 + _page_specs((PAGE, HEAD_DIM), layer, n_pages, False)
                 + _page_specs((PAGE, HEAD_DIM), layer, n_pages, False),
        out_specs=_per_seq((t, MIX_W)),
        scratch_shapes=[pltpu.VMEM((CHUNK, HEAD_DIM), F32), pltpu.VMEM((CHUNK, HEAD_DIM), F32),
                        pltpu.VMEM((rows, 1), F32), pltpu.VMEM((rows, 1), F32), pltpu.VMEM((rows, HEAD_DIM), F32)])
    return pl.pallas_call(
        functools.partial(_dsa_attend_kernel, t=t),
        grid_spec=grid_spec,
        out_shape=jax.ShapeDtypeStruct((b, t, MIX_W), F32),
        compiler_params=_cparams("parallel", "arbitrary"),
        name="dsa_sample_attend",
    )(page_table, q, kn, vn, bias, bias, *([cache_k] * PAGES_PER_STEP), *([cache_v] * PAGES_PER_STEP))


def _pad_w_in_t(w):
    wt = w.T
    widths = (512, 512, 512, 4, 512, 512, 512, 512, 128, 128, 512, 64, 8, 512)
    names = ("qa", "ka", "va", "fa", "qb", "kb", "vb", "qc", "kc", "vc", "qi", "ki", "wi", "xd")
    seg, off = {}, 0
    for name, width in zip(names, widths):
        seg[name] = wt[off:off + width]
        off += width
    order = ("qa", "ka", "va", "qb", "kb", "vb", "qc", "qi", "xd", "kc", "vc", "ki", "fa", "wi")
    rows = jnp.concatenate([seg[n] for n in order], axis=0)
    return jnp.pad(rows, ((0, N_PROJ - rows.shape[0]), (0, 0))).astype(BF16)


def _cols(p, off, width):
    return p[..., off:off + width]


def _tail_cols(tail32, off, width):
    return _cols(tail32, off - OFF_XD, width)


def _new_rows(tail32, head_rows, lf, b, t):
    ka, va, kb, vb = (r.reshape(b, t, N_HEADS, HEAD_DIM) for r in head_rows)
    return (ka, va, _cols(lf, MISC_FA, N_HEADS), kb, vb, _tail_cols(tail32, OFF_KC, HEAD_DIM),
            _tail_cols(tail32, OFF_VC, HEAD_DIM), _tail_cols(tail32, OFF_MISC + MISC_KI, IDX_DIM))


def _dense_tail(x2d, h16, ys, lw):
    merged = merge(h16, ys, lw["w_gate"], lw["w_branch"])
    x2d = matmul_residual(merged, lw["w_out"], x2d)
    return ffn(x2d, lw["g_ffn"], lw["w_fg"], lw["w_fu"], lw["w_fd"])


def _forget_bias_row(b_f):
    return jnp.zeros((1, LANES), F32).at[0, MISC_FA:MISC_FA + N_HEADS].set(b_f)


def _conv_tail(conv_state, tail32):
    xd = _tail_cols(tail32, OFF_XD, MIX_W)
    return jnp.concatenate([conv_state, xd], axis=1)[:, -(CONV_W - 1):]


def _prompt_layer(x2d, b, t, lw, topk):
    tail32, p16, h16, *head_rows = norm_proj(x2d, lw["g_mix"], lw["w_in_t"])
    tail32, p16 = tail32.reshape(b, t, N_TAIL), p16.reshape(b, t, N_PROJ)
    lf, fc = logf_cumsum(_tail_cols(tail32, OFF_MISC, LANES), _forget_bias_row(lw["b_f"]))
    tp = pl.cdiv(t, CHUNK) * CHUNK
    frow = jnp.pad(_cols(fc, MISC_FA, N_HEADS), ((0, 0), (0, tp - t), (0, 0))).transpose(0, 2, 1)
    conv0 = jnp.zeros((b, CONV_W - 1, MIX_W), F32)
    ya = fox_prompt(p16, frow.reshape(b, N_HEADS, 1, tp))
    yb = sb_prompt(p16)
    yc = dsa_prompt(tail32, p16, topk)
    yd, h_last = rglru(tail32, conv0, jnp.zeros((b, MIX_W), F32), lw, BF16)
    ys = [y.reshape(b * t, MIX_W) for y in (ya, yb, yc, yd)]
    rows = _new_rows(tail32, head_rows, lf, b, t) + (_conv_tail(conv0, tail32), h_last.reshape(b, MIX_W))
    return _dense_tail(x2d, h16, ys, lw), rows


def _sample_layer(x2d, b, t, lw, topk, layer, page_table, caches, conv_state, h0):
    ca_k, ca_v, ca_lft, cb_k, cb_v, cc_k, cc_v, cc_idx_t = caches
    n_past = page_table.shape[1] * PAGE
    tail32, p16, h16, *head_rows = norm_proj(x2d, lw["g_mix"], lw["w_in_t"])
    tail32 = tail32.reshape(b, t, N_TAIL)
    seg = lambda off, width: _cols(p16, off, width).astype(F32).reshape(b, t, width)
    misc = _tail_cols(tail32, OFF_MISC, LANES)
    lf, fc = logf_cumsum(misc, _forget_bias_row(lw["b_f"]))
    cn = _cols(fc, MISC_FA, N_HEADS)
    cnrow = jnp.pad(cn.transpose(0, 2, 1), ((0, 0), (0, 0), (0, CHUNK - t)))
    ya = fox_sample(page_table, layer, seg(OFF_QA, MIX_W), seg(OFF_KA, MIX_W), seg(OFF_VA, MIX_W),
                    cn, cnrow, ca_k, ca_v, ca_lft)
    yb = sb_sample(page_table, layer, seg(OFF_QB, MIX_W), seg(OFF_KB, MIX_W), seg(OFF_VB, MIX_W), cb_k, cb_v)
    score_past, score_new = dsa_sample_scores(page_table, layer, seg(OFF_QI, MIX_W), misc, cc_idx_t)
    scores = jnp.concatenate([score_past, score_new], axis=-1).reshape(b * t, n_past + CHUNK)
    bias = dsa_sample_select(scores, t, n_past, topk).reshape(b, t, n_past + CHUNK)
    yc = dsa_sample_attend(page_table, layer, seg(OFF_QC, MIX_W), seg(OFF_KC, HEAD_DIM), seg(OFF_VC, HEAD_DIM),
                           bias, cc_k, cc_v)
    yd, h_last = rglru(tail32, conv_state, h0, lw, F32)
    ys = [y.reshape(b * t, MIX_W).astype(BF16) for y in (ya, yb, yc, yd)]
    rows = _new_rows(tail32, head_rows, lf, b, t) + (_conv_tail(conv_state, tail32), h_last.reshape(b, MIX_W))
    return _dense_tail(x2d, h16, ys, lw), rows


def kernel(x_prompt, x_sample, cache_a_k, cache_a_v, cache_a_logf, cache_b_k, cache_b_v, cache_c_k, cache_c_v, cache_c_idx_k, state_d_conv, state_d_h, page_table, meta_tokens, w_in, b_forget, conv_w, conv_b, w_rg_a, b_rg_a, w_rg_x, b_rg_x, rg_lambda, w_gate, w_branch, w_out, norm_mix, norm_ffn, w_ffn_gate, w_ffn_up, w_ffn_down, norm_final):
    depth = w_in.shape[0]
    n_p, seq_p, _ = x_prompt.shape
    n_dec, t_dec, _ = x_sample.shape
    n_pool = cache_a_k.shape[1]
    past_len = page_table.shape[1] * PAGE
    topk_p = min(TOPK_MAX, seq_p // 4)
    topk_s = min(TOPK_MAX, (past_len + t_dec) // 4)

    rows_view = lambda c: c.reshape(depth, n_pool, PAGE * N_HEADS, HEAD_DIM)
    caches = (rows_view(cache_a_k), rows_view(cache_a_v), cache_a_logf.transpose(0, 1, 3, 2),
              rows_view(cache_b_k), rows_view(cache_b_v), cache_c_k, cache_c_v, cache_c_idx_k.transpose(0, 1, 3, 2))

    xp = jnp.concatenate([jnp.broadcast_to(meta_tokens[None], (n_p, N_META, D_MODEL)), x_prompt], axis=1)
    t_p = seq_p + N_META
    xp = xp.reshape(n_p * t_p, D_MODEL)
    xs = x_sample.reshape(n_dec * t_dec, D_MODEL)

    prompt_rows, sample_rows = [], []
    for l in range(depth):
        lw = dict(w_in_t=_pad_w_in_t(w_in[l]), b_f=b_forget[l], conv_w=conv_w[l], conv_b=conv_b[l],
                  w_a=_block_diag(w_rg_a[l]).astype(BF16), b_a=b_rg_a[l],
                  w_x=_block_diag(w_rg_x[l]).astype(BF16), b_x=b_rg_x[l], lam=rg_lambda[l],
                  w_gate=w_gate[l].astype(BF16), w_branch=w_branch[l].astype(BF16), w_out=w_out[l].astype(BF16),
                  g_mix=norm_mix[l], g_ffn=norm_ffn[l], w_fg=w_ffn_gate[l].astype(BF16),
                  w_fu=w_ffn_up[l].astype(BF16), w_fd=w_ffn_down[l].astype(BF16))
        xp, rows = _prompt_layer(xp, n_p, t_p, lw, topk_p)
        prompt_rows.append(rows)
        xs, rows = _sample_layer(xs, n_dec, t_dec, lw, topk_s, l, page_table, caches, state_d_conv[l], state_d_h[l])
        sample_rows.append(rows)

    y_prompt = rmsnorm(xp, norm_final).reshape(n_p, t_p, D_MODEL)[:, N_META:]
    y_sample = rmsnorm(xs, norm_final).reshape(n_dec, t_dec, D_MODEL)
    stack = lambda rows: tuple(jnp.stack(list(r)) for r in zip(*rows))
    return (y_prompt, y_sample) + stack(prompt_rows) + stack(sample_rows)
```

```python
import functools

import jax
import jax.numpy as jnp
from jax import lax
from jax.experimental import pallas as pl
from jax.experimental.pallas import tpu as pltpu

F32 = jnp.float32
BF16 = jnp.bfloat16
I32 = jnp.int32

D_MODEL = 2048
N_META = 16
HEAD_DIM = 128
N_HEADS = 4
MIX_W = 512
IDX_HEADS = 8
IDX_DIM = 64
TOPK_MAX = 256
RG_C = 8.0
CONV_W = 4
RMS_EPS = 1e-6
PAGE = 128

LANES = 128
SUBLANES = 8
CHUNK = 128
BAND_TILES = 2
HEADS_PER_ITER = 2
VMEM_LIMIT = 56 * 1024 * 1024

OFF_QA, OFF_KA, OFF_VA = 0, 512, 1024
OFF_QB, OFF_KB, OFF_VB = 1536, 2048, 2560
OFF_QC, OFF_QI, OFF_XD = 3072, 3584, 4096
OFF_KC, OFF_VC, OFF_MISC = 4608, 4736, 4864
MISC_KI, MISC_FA, MISC_WI = 0, 64, 68
N_PROJ = 5120

NEG_INF = float("-inf")
INT_MIN = -2 ** 31


def _cparams(*sem):
    return pltpu.CompilerParams(dimension_semantics=sem, vmem_limit_bytes=VMEM_LIMIT)


def _split_bf16(x, pieces):
    out, rest = [], x
    for n in range(pieces):
        part = rest.astype(BF16)
        out.append(part)
        if n + 1 < pieces:
            rest = rest - part.astype(F32)
    return out


def _dot01_left(m01, x, pieces=3):
    return sum(jnp.dot(m01, p, preferred_element_type=F32) for p in _split_bf16(x, pieces))


def _dot01_right(x, m01, pieces=3):
    return sum(jnp.dot(p, m01, preferred_element_type=F32) for p in _split_bf16(x, pieces))


def _dot_nt(a, b):
    return lax.dot_general(a, b, (((1,), (1,)), ((), ())), preferred_element_type=F32)


def _softplus(x, log1p=jnp.log1p):
    return jnp.maximum(x, 0.0) + log1p(jnp.exp(-jnp.abs(x)))


def _softplus_abs(x):
    return _softplus(x, lambda e: jnp.log(1.0 + e))


SB_SPLIT = 2


def _iota(shape, dim):
    return lax.broadcasted_iota(I32, shape, dim)


def _row_tile(m, pref):
    for t in range(min(pref, m), 7, -1):
        if m % t == 0 and t % SUBLANES == 0:
            return t
    raise ValueError(f"no row tile for {m}")


PROJ_TN = 512
ROW_TILES = tuple(off // PROJ_TN for off in (OFF_KA, OFF_VA, OFF_KB, OFF_VB))
TAIL_TILE0 = OFF_XD // PROJ_TN
N_TAIL = N_PROJ - OFF_XD


def _norm_proj_kernel(x_ref, g_ref, wt_ref, *rest):
    tail_ref, o16_ref, h16_ref, ka_ref, va_ref, kb_ref, vb_ref, h_scr = rest[-8:]
    j = pl.program_id(1)
    tm = x_ref.shape[0]

    @pl.when(j == 0)
    def _():
        x = x_ref[...]
        y = x * lax.rsqrt(jnp.mean(x * x, axis=-1, keepdims=True) + RMS_EPS)
        h = (y * g_ref[...]).astype(BF16)
        h_scr[...] = h
        h16_ref[...] = h

    acc = _dot_nt(h_scr[...], wt_ref[...])
    o16_ref[...] = acc.astype(BF16)

    @pl.when(j >= TAIL_TILE0)
    def _():
        tail_ref[...] = acc

    for tile, rows_ref in zip(ROW_TILES, (ka_ref, va_ref, kb_ref, vb_ref)):
        @pl.when(j == tile)
        def _(rows_ref=rows_ref):
            for h in range(N_HEADS):
                rows_ref.at[0][pl.ds(h, tm, stride=N_HEADS), :] = acc[:, h * HEAD_DIM:(h + 1) * HEAD_DIM]


def norm_proj(x, g, wt16, layer, depth, prev_rows, tm_pref=1032):
    m, d = x.shape
    n = wt16.shape[0]
    tm = _row_tile(m, tm_pref)
    rows_spec = pl.BlockSpec((1, tm * N_HEADS, HEAD_DIM), lambda i, j: (layer, i, 0))
    rows_shape = jax.ShapeDtypeStruct((depth, m * N_HEADS, HEAD_DIM), F32)
    prev = list(prev_rows) if prev_rows is not None else []
    return pl.pallas_call(
        _norm_proj_kernel,
        grid=(m // tm, n // PROJ_TN),
        in_specs=[pl.BlockSpec((tm, d), lambda i, j: (i, 0), pipeline_mode=pl.Buffered(1)),
                  pl.BlockSpec((1, d), lambda i, j: (0, 0)),
                  pl.BlockSpec((PROJ_TN, d), lambda i, j: (j, 0))] + [pl.BlockSpec(memory_space=pl.ANY)] * len(prev),
        out_specs=[pl.BlockSpec((tm, PROJ_TN), lambda i, j: (i, jnp.maximum(j - TAIL_TILE0, 0))),
                   pl.BlockSpec((tm, PROJ_TN), lambda i, j: (i, j)),
                   pl.BlockSpec((tm, d), lambda i, j: (i, 0)),
                   rows_spec, rows_spec, rows_spec, rows_spec],
        out_shape=[jax.ShapeDtypeStruct((m, N_TAIL), F32),
                   jax.ShapeDtypeStruct((m, n), BF16),
                   jax.ShapeDtypeStruct((m, d), BF16),
                   rows_shape, rows_shape, rows_shape, rows_shape],
        input_output_aliases={3 + k: 3 + k for k in range(len(prev))},
        scratch_shapes=[pltpu.VMEM((tm, d), BF16)],
        compiler_params=_cparams("parallel", "arbitrary"),
        name="norm_proj",
    )(x, g.reshape(1, d), wt16, *prev)


def _logf_kernel(misc_ref, bias_ref, logf_ref, fcum_ref, pad_scr, *, t, n_chunks):
    lane = _iota((1, LANES), 1)
    live = (lane >= MISC_FA) & (lane < MISC_FA + N_HEADS)
    logf = jnp.where(live, jax.nn.log_sigmoid(misc_ref[0] + bias_ref[...]), 0.0)
    logf_ref[0] = logf
    pad_scr[pl.ds(0, t), :] = logf
    if n_chunks * CHUNK > t:
        pad_scr[pl.ds(t, n_chunks * CHUNK - t), :] = jnp.zeros((n_chunks * CHUNK - t, LANES), F32)
    tril = (_iota((CHUNK, CHUNK), 1) <= _iota((CHUNK, CHUNK), 0)).astype(BF16)
    carry = jnp.zeros((1, LANES), F32)
    for c in range(n_chunks):
        f = _dot01_left(tril, pad_scr[pl.ds(c * CHUNK, CHUNK), :]) + carry
        rows = min(CHUNK, t - c * CHUNK)
        fcum_ref[0, pl.ds(c * CHUNK, rows), :] = f[:rows]
        carry = f[CHUNK - 1:CHUNK, :]


def logf_cumsum(misc, bias_row):
    b, t, _ = misc.shape
    n_chunks = pl.cdiv(t, CHUNK)
    return pl.pallas_call(
        functools.partial(_logf_kernel, t=t, n_chunks=n_chunks),
        grid=(b,),
        in_specs=[pl.BlockSpec((1, t, LANES), lambda i: (i, 0, 0)),
                  pl.BlockSpec((1, LANES), lambda i: (0, 0))],
        out_specs=[pl.BlockSpec((1, t, LANES), lambda i: (i, 0, 0)),
                   pl.BlockSpec((1, t, LANES), lambda i: (i, 0, 0))],
        out_shape=[jax.ShapeDtypeStruct((b, t, LANES), F32)] * 2,
        scratch_shapes=[pltpu.VMEM((n_chunks * CHUNK, LANES), F32)],
        compiler_params=_cparams("parallel"),
        name="logf_cumsum",
    )(misc, bias_row)


def _pad_copy(dst, src, t):
    rows = dst.shape[0]
    dst[pl.ds(0, t), :] = src
    if rows > t:
        dst[pl.ds(t, rows - t), :] = jnp.zeros((rows - t, dst.shape[1]), dst.dtype)


def _stage_heads(dst, src_ref, t):
    for h in range(N_HEADS):
        _pad_copy(dst.at[h], src_ref[0, :, h * HEAD_DIM:(h + 1) * HEAD_DIM], t)


def _unstage_heads(o_ref, src, t):
    for h in range(N_HEADS):
        o_ref[0, :, h * HEAD_DIM:(h + 1) * HEAD_DIM] = src[h, 0:t, :]


def _bands(n_tiles):
    return [(lo, min(lo + BAND_TILES, n_tiles)) for lo in range(0, n_tiles, BAND_TILES)]


def _for_heads_and_tiles(n_tiles, tile):
    for lo, hi in _bands(n_tiles):
        def per_head_pair(hp, _, lo=lo, hi=hi):
            def per_tile(i, _):
                for s in range(HEADS_PER_ITER):
                    tile(hp * HEADS_PER_ITER + s, i, hi * CHUNK)
                return 0
            return lax.fori_loop(lo, hi, per_tile, 0)
        lax.fori_loop(0, N_HEADS // HEADS_PER_ITER, per_head_pair, 0)


def _fox_prompt_kernel(q_ref, k_ref, v_ref, frow_ref, o_ref, q_scr, k_scr, v_scr, o_scr, *, t):
    scale = HEAD_DIM ** -0.5
    _stage_heads(q_scr, q_ref, t)
    _stage_heads(k_scr, k_ref, t)
    _stage_heads(v_scr, v_ref, t)

    def tile(h, i, width):
        row0 = pl.multiple_of(i * CHUNK, CHUNK)
        q = q_scr[h, pl.ds(row0, CHUNK), :]
        s = _dot_nt(q, k_scr[h, 0:width, :]) * scale - frow_ref[0, h, :, 0:width]
        vis = _iota((CHUNK, width), 1) <= row0 + _iota((CHUNK, width), 0)
        s = jnp.where(vis, s, NEG_INF)
        p = jnp.exp(s - jnp.max(s, axis=1, keepdims=True))
        l = jnp.sum(p, axis=1, keepdims=True)
        o = jnp.dot(p.astype(BF16), v_scr[h, 0:width, :], preferred_element_type=F32) / l
        o_scr[h, pl.ds(row0, CHUNK), :] = o.astype(o_scr.dtype)

    _for_heads_and_tiles(k_scr.shape[1] // CHUNK, tile)
    _unstage_heads(o_ref, o_scr, t)


def _head_scratch(tp, n):
    return [pltpu.VMEM((N_HEADS, tp, HEAD_DIM), BF16)] * n


def fox_prompt(proj16, frow):
    b, t, _ = proj16.shape
    tp = frow.shape[-1]
    blk = lambda off: pl.BlockSpec((1, t, MIX_W), lambda i, o=off // MIX_W: (i, 0, o))
    return pl.pallas_call(
        functools.partial(_fox_prompt_kernel, t=t),
        grid=(b,),
        in_specs=[blk(OFF_QA), blk(OFF_KA), blk(OFF_VA),
                  pl.BlockSpec((1, N_HEADS, 1, tp), lambda i: (i, 0, 0, 0))],
        out_specs=pl.BlockSpec((1, t, MIX_W), lambda i: (i, 0, 0)),
        out_shape=jax.ShapeDtypeStruct((b, t, MIX_W), BF16),
        scratch_shapes=_head_scratch(tp, 4),
        compiler_params=_cparams("parallel"),
        name="fox_prompt",
    )(proj16, proj16, proj16, frow)


def _suffix_matrix():
    return (_iota((CHUNK, CHUNK), 0) > _iota((CHUNK, CHUNK), 1)).astype(BF16)


def _sb_prompt_kernel(q_ref, k_ref, v_ref, o_ref, q_scr, k_scr, v_scr, o_scr, *, t):
    scale = HEAD_DIM ** -0.5
    _stage_heads(q_scr, q_ref, t)
    _stage_heads(k_scr, k_ref, t)
    _stage_heads(v_scr, v_ref, t)
    suffix = _suffix_matrix()

    def tile(h, i, width):
        row0 = pl.multiple_of(i * CHUNK, CHUNK)
        q = q_scr[h, pl.ds(row0, CHUNK), :]
        z = _dot_nt(q, k_scr[h, 0:width, :]) * scale
        vis = _iota((CHUNK, width), 1) < row0 + _iota((CHUNK, width), 0)
        log_keep = jnp.where(vis, -_softplus_abs(z), 0.0)
        later_chunks = jnp.zeros((CHUNK, 1), F32)
        pieces = [None] * (width // CHUNK)
        for c in reversed(range(width // CHUNK)):
            cs = slice(c * CHUNK, (c + 1) * CHUNK)
            later = _dot01_right(log_keep[:, cs], suffix, SB_SPLIT) + later_chunks
            att = jnp.where(vis[:, cs], jnp.exp(log_keep[:, cs] + z[:, cs] + later), 0.0)
            pieces[c] = att.astype(BF16)
            later_chunks = later_chunks + jnp.sum(log_keep[:, cs], axis=1, keepdims=True)
        att = jnp.concatenate(pieces, axis=1)
        o = jnp.dot(att, v_scr[h, 0:width, :], preferred_element_type=F32)
        o_scr[h, pl.ds(row0, CHUNK), :] = o.astype(o_scr.dtype)

    _for_heads_and_tiles(k_scr.shape[1] // CHUNK, tile)
    _unstage_heads(o_ref, o_scr, t)


def sb_prompt(proj16):
    b, t, _ = proj16.shape
    tp = pl.cdiv(t, CHUNK) * CHUNK
    blk = lambda off: pl.BlockSpec((1, t, MIX_W), lambda i, o=off // MIX_W: (i, 0, o))
    return pl.pallas_call(
        functools.partial(_sb_prompt_kernel, t=t),
        grid=(b,),
        in_specs=[blk(OFF_QB), blk(OFF_KB), blk(OFF_VB)],
        out_specs=pl.BlockSpec((1, t, MIX_W), lambda i: (i, 0, 0)),
        out_shape=jax.ShapeDtypeStruct((b, t, MIX_W), BF16),
        scratch_shapes=_head_scratch(tp, 4),
        compiler_params=_cparams("parallel"),
        name="sb_prompt",
    )(proj16, proj16, proj16)


def _order_key(score):
    score = jnp.where(score == 0.0, 0.0, score)
    bits = lax.bitcast_convert_type(score, I32)
    return bits ^ ((bits >> 31) & 0x7FFFFFFF)


def _select_topk(key_ref, bias_ref, vis_fn, rows, width, topk):
    kf = float(topk)

    def count(pred):
        return jnp.sum(jnp.where(pred(key_ref[0:rows, 0:width]), 1.0, 0.0), axis=1, keepdims=True)

    thr0 = jnp.where(count(lambda k: k >= 0) >= kf, 0, INT_MIN).astype(I32)

    def bit_step(b, thr):
        cand = thr | (jnp.int32(1) << (30 - b))
        return jnp.where(count(lambda k: k >= cand) >= kf, cand, thr)

    thr = lax.fori_loop(0, 31, bit_step, thr0)
    need = kf - count(lambda k: k > thr)
    keys = key_ref[0:rows, 0:width]
    vis = vis_fn(0, width)
    n_tie = jnp.sum(jnp.where((keys == thr) & vis, 1.0, 0.0), axis=1, keepdims=True)
    bias_ref[0:rows, 0:width] = jnp.where((keys >= thr) & vis, 0.0, NEG_INF)

    @pl.when(jnp.max(n_tie - need) > 0.0)
    def _():
        prefix = (_iota((CHUNK, CHUNK), 0) <= _iota((CHUNK, CHUNK), 1)).astype(BF16)
        seen = jnp.zeros((rows, 1), F32)
        for c in range(width // CHUNK):
            k = key_ref[0:rows, c * CHUNK:(c + 1) * CHUNK]
            v = vis_fn(c * CHUNK, CHUNK)
            tie = jnp.where((k == thr) & v, 1.0, 0.0)
            rank = jnp.dot(tie.astype(BF16), prefix, preferred_element_type=F32) + seen
            sel = ((k > thr) & v) | ((tie > 0.0) & (rank <= need))
            bias_ref[0:rows, c * CHUNK:(c + 1) * CHUNK] = jnp.where(sel, 0.0, NEG_INF)
            seen = seen + jnp.sum(tie, axis=1, keepdims=True)


def _stack_heads(q):
    return jnp.concatenate([q[:, h * HEAD_DIM:(h + 1) * HEAD_DIM] for h in range(N_HEADS)], axis=0)


def _dsa_prompt_kernel(q_ref, qi_ref, kc_ref, vc_ref, misc32_ref, misc16_ref, o_ref,
                       kc_scr, vc_scr, ki_scr, key_scr, bias_scr, *, t, topk):
    n_full, tail = t // CHUNK, t % CHUNK
    scale = HEAD_DIM ** -0.5
    idx_scale = (IDX_DIM * IDX_HEADS) ** -0.5
    _pad_copy(kc_scr, kc_ref[0], t)
    _pad_copy(vc_scr, vc_ref[0], t)
    _pad_copy(ki_scr, misc16_ref[0], t)

    def tile(row0, tq, width):
        def vis_fn(c0, w):
            return (c0 + _iota((tq, w), 1)) <= (row0 + _iota((tq, w), 0))

        qi = qi_ref[0, pl.ds(row0, tq), :]
        wi = misc32_ref[0, pl.ds(row0, tq), MISC_WI:MISC_WI + IDX_HEADS]
        ki = ki_scr[0:width, MISC_KI:MISC_KI + IDX_DIM]
        score = None
        for h in range(IDX_HEADS):
            term = wi[:, h:h + 1] * jnp.maximum(_dot_nt(qi[:, h * IDX_DIM:(h + 1) * IDX_DIM], ki), 0.0)
            score = term if score is None else score + term
        key_scr[0:tq, 0:width] = jnp.where(vis_fn(0, width), _order_key(score * idx_scale), INT_MIN)
        _select_topk(key_scr, bias_scr, vis_fn, tq, width, topk)

        q4 = _stack_heads(q_ref[0, pl.ds(row0, tq), :])
        bias = bias_scr[0:tq, 0:width]
        s = _dot_nt(q4, kc_scr[0:width, :]) * scale + jnp.concatenate([bias] * N_HEADS, axis=0)
        p = jnp.exp(s - jnp.max(s, axis=1, keepdims=True))
        l = jnp.sum(p, axis=1, keepdims=True)
        out = jnp.dot(p.astype(BF16), vc_scr[0:width, :], preferred_element_type=F32) / l
        for h in range(N_HEADS):
            o_ref[0, pl.ds(row0, tq), h * HEAD_DIM:(h + 1) * HEAD_DIM] = out[h * tq:(h + 1) * tq].astype(o_ref.dtype)

    for lo, hi in _bands(n_full):
        tile(lo * CHUNK, (hi - lo) * CHUNK, hi * CHUNK)
    if tail:
        tile(n_full * CHUNK, tail, (n_full + 1) * CHUNK)


def dsa_prompt(tail32, proj16, topk):
    b, t, _ = proj16.shape
    tp = pl.cdiv(t, CHUNK) * CHUNK
    wide = lambda off: pl.BlockSpec((1, t, MIX_W), lambda i, o=off // MIX_W: (i, 0, o))
    narrow = lambda off: pl.BlockSpec((1, t, LANES), lambda i, o=off // LANES: (i, 0, o))
    return pl.pallas_call(
        functools.partial(_dsa_prompt_kernel, t=t, topk=topk),
        grid=(b,),
        in_specs=[wide(OFF_QC), wide(OFF_QI), narrow(OFF_KC), narrow(OFF_VC), narrow(OFF_MISC - OFF_XD), narrow(OFF_MISC)],
        out_specs=pl.BlockSpec((1, t, MIX_W), lambda i: (i, 0, 0)),
        out_shape=jax.ShapeDtypeStruct((b, t, MIX_W), BF16),
        scratch_shapes=[pltpu.VMEM((tp, HEAD_DIM), BF16), pltpu.VMEM((tp, HEAD_DIM), BF16),
                        pltpu.VMEM((tp, LANES), BF16),
                        pltpu.VMEM((BAND_TILES * CHUNK, tp), I32), pltpu.VMEM((BAND_TILES * CHUNK, tp), F32)],
        compiler_params=_cparams("parallel"),
        name="dsa_prompt",
    )(proj16, proj16, proj16, proj16, tail32, proj16)


CONV_PAD = 8


def _rglru_kernel(xd_ref, cs_ref, h0_ref, cw_ref, cb_ref, wa_ref, ba_ref, wx_ref, bx_ref, lam_ref,
                  y_ref, hl_ref, xpad_scr, a_scr, u_scr, hs_scr, *, t):
    n_full, tail = t // CHUNK, t % CHUNK
    xpad_scr[pl.ds(0, CONV_PAD), :] = jnp.zeros((CONV_PAD, MIX_W), F32)
    xpad_scr[pl.ds(CONV_PAD - (CONV_W - 1), CONV_W - 1), :] = cs_ref[0]
    xpad_scr[pl.ds(CONV_PAD, t), :] = xd_ref[0]
    decay_rate = -RG_C * _softplus(-lam_ref[...])

    def chunk(r0, rows, h):
        win = xpad_scr[pl.ds(r0, rows + CONV_PAD), :]
        xc = cb_ref[...]
        for i in range(CONV_W):
            lo = CONV_PAD - (CONV_W - 1) + i
            xc = xc + win[lo:lo + rows] * cw_ref[i:i + 1, :]
        xc16 = xc.astype(BF16)
        gate_r = jax.nn.sigmoid(jnp.dot(xc16, wa_ref[...], preferred_element_type=F32) + ba_ref[...])
        gate_i = jax.nn.sigmoid(jnp.dot(xc16, wx_ref[...], preferred_element_type=F32) + bx_ref[...])
        log_a = gate_r * decay_rate
        a_scr[0:rows, :] = jnp.exp(log_a)
        th = jnp.tanh(log_a)
        u_scr[0:rows, :] = jnp.sqrt(-2.0 * th / (1.0 - th)) * gate_i * xc

        def step(r, hh):
            hh = a_scr[pl.ds(r, 1), :] * hh + u_scr[pl.ds(r, 1), :]
            hs_scr[pl.ds(r, 1), :] = hh
            return hh
        h = lax.fori_loop(0, rows, step, h, unroll=8)
        y_ref[0, pl.ds(r0, rows), :] = hs_scr[0:rows, :].astype(y_ref.dtype)
        return h

    h = h0_ref[0]
    if n_full:
        h = lax.fori_loop(0, n_full, lambda i, hh: chunk(pl.multiple_of(i * CHUNK, CHUNK), CHUNK, hh), h)
    if tail:
        h = chunk(n_full * CHUNK, tail, h)
    hl_ref[0] = h


def rglru(tail32, conv_state, h0, lw, out_dtype):
    b, t, _ = tail32.shape
    conv_w, conv_b, wa16, b_a, wx16, b_x, lam = (lw[k] for k in ("conv_w", "conv_b", "w_a", "b_a", "w_x", "b_x", "lam"))
    row = lambda v: v.reshape(1, MIX_W)
    const = lambda shape: pl.BlockSpec(shape, lambda i: (0,) * len(shape))
    return pl.pallas_call(
        functools.partial(_rglru_kernel, t=t),
        grid=(b,),
        in_specs=[pl.BlockSpec((1, t, MIX_W), lambda i: (i, 0, 0)),
                  pl.BlockSpec((1, CONV_W - 1, MIX_W), lambda i: (i, 0, 0)),
                  pl.BlockSpec((1, 1, MIX_W), lambda i: (i, 0, 0)),
                  const((CONV_W, MIX_W)), const((1, MIX_W)),
                  const((MIX_W, MIX_W)), const((1, MIX_W)),
                  const((MIX_W, MIX_W)), const((1, MIX_W)), const((1, MIX_W))],
        out_specs=[pl.BlockSpec((1, t, MIX_W), lambda i: (i, 0, 0)),
                   pl.BlockSpec((1, 1, MIX_W), lambda i: (i, 0, 0))],
        out_shape=[jax.ShapeDtypeStruct((b, t, MIX_W), out_dtype), jax.ShapeDtypeStruct((b, 1, MIX_W), F32)],
        scratch_shapes=[pltpu.VMEM((CONV_PAD + pl.cdiv(t, CHUNK) * CHUNK + CONV_PAD, MIX_W), F32),
                        pltpu.VMEM((CHUNK, MIX_W), F32), pltpu.VMEM((CHUNK, MIX_W), F32),
                        pltpu.VMEM((CHUNK, MIX_W), F32)],
        compiler_params=_cparams("parallel"),
        name="rglru",
    )(tail32, conv_state, h0.reshape(b, 1, MIX_W), conv_w, row(conv_b), wa16, row(b_a), wx16, row(b_x), row(lam))


def _block_diag(w):
    n, c, e = w.shape
    eye = jnp.eye(n, dtype=w.dtype)
    return (eye[:, None, :, None] * w[:, :, None, :]).reshape(n * c, n * e)


def _merge_kernel(h_ref, ya_ref, yb_ref, yc_ref, yd_ref, g0_ref, g1_ref, g2_ref, g3_ref, wb_ref, o_ref):
    h = h_ref[...]
    acc = None
    for n, (y_ref, g_ref) in enumerate(((ya_ref, g0_ref), (yb_ref, g1_ref), (yc_ref, g2_ref), (yd_ref, g3_ref))):
        gate = jax.nn.sigmoid(jnp.dot(h, g_ref[...], preferred_element_type=F32))
        term = gate * jnp.dot(y_ref[...], wb_ref[n], preferred_element_type=F32)
        acc = term if acc is None else acc + term
    o_ref[...] = acc.astype(o_ref.dtype)


def merge(h16, ys, wg16, wb16, tm_pref=512, tn=512):
    m, d = h16.shape
    tm = _row_tile(m, tm_pref)
    nj = d // tn
    y_spec = pl.BlockSpec((tm, MIX_W), lambda i, j: (i, 0))
    g_spec = lambda n: pl.BlockSpec((d, tn), lambda i, j, n=n: (0, n * nj + j))
    return pl.pallas_call(
        _merge_kernel,
        grid=(m // tm, nj),
        in_specs=[pl.BlockSpec((tm, d), lambda i, j: (i, 0)), y_spec, y_spec, y_spec, y_spec,
                  g_spec(0), g_spec(1), g_spec(2), g_spec(3),
                  pl.BlockSpec((4, MIX_W, tn), lambda i, j: (0, 0, j))],
        out_specs=pl.BlockSpec((tm, tn), lambda i, j: (i, j)),
        out_shape=jax.ShapeDtypeStruct((m, d), BF16),
        compiler_params=_cparams("parallel", "arbitrary"),
        name="merge",
    )(h16, *ys, wg16, wg16, wg16, wg16, wb16)


def _matmul_res_kernel(a_ref, w_ref, x_ref, o_ref):
    o_ref[...] = x_ref[...] + jnp.dot(a_ref[...], w_ref[...], preferred_element_type=F32)


def matmul_residual(a16, w16, x, tm_pref=344):
    m, k = a16.shape
    n = w16.shape[1]
    tm = _row_tile(m, tm_pref)
    return pl.pallas_call(
        _matmul_res_kernel,
        grid=(m // tm,),
        in_specs=[pl.BlockSpec((tm, k), lambda i: (i, 0)),
                  pl.BlockSpec((k, n), lambda i: (0, 0)),
                  pl.BlockSpec((tm, n), lambda i: (i, 0))],
        out_specs=pl.BlockSpec((tm, n), lambda i: (i, 0)),
        out_shape=jax.ShapeDtypeStruct((m, n), F32),
        compiler_params=_cparams("parallel"),
        name="out_proj",
    )(a16, w16, x)


def _ffn_kernel(x_ref, g_ref, wg_ref, wu_ref, wd_ref, o_ref, h_scr, acc_scr):
    f = pl.program_id(1)

    @pl.when(f == 0)
    def _():
        x = x_ref[...]
        y = x * lax.rsqrt(jnp.mean(x * x, axis=-1, keepdims=True) + RMS_EPS)
        h_scr[...] = (y * g_ref[...]).astype(BF16)
        acc_scr[...] = jnp.zeros_like(acc_scr)

    h = h_scr[...]
    gate = jnp.dot(h, wg_ref[...], preferred_element_type=F32)
    up = jnp.dot(h, wu_ref[...], preferred_element_type=F32)
    act = (jax.nn.silu(gate) * up).astype(BF16)
    acc_scr[...] += jnp.dot(act, wd_ref[...], preferred_element_type=F32)

    @pl.when(f == pl.num_programs(1) - 1)
    def _():
        o_ref[...] = x_ref[...] + acc_scr[...]


def ffn(x, g, wg16, wu16, wd16, tm_pref=688, tf=512):
    m, d = x.shape
    dff = wg16.shape[1]
    tm = _row_tile(m, tm_pref)
    return pl.pallas_call(
        _ffn_kernel,
        grid=(m // tm, dff // tf),
        in_specs=[pl.BlockSpec((tm, d), lambda i, f: (i, 0)),
                  pl.BlockSpec((1, d), lambda i, f: (0, 0)),
                  pl.BlockSpec((d, tf), lambda i, f: (0, f)),
                  pl.BlockSpec((d, tf), lambda i, f: (0, f)),
                  pl.BlockSpec((tf, d), lambda i, f: (f, 0))],
        out_specs=pl.BlockSpec((tm, d), lambda i, f: (i, 0)),
        out_shape=jax.ShapeDtypeStruct((m, d), F32),
        scratch_shapes=[pltpu.VMEM((tm, d), BF16), pltpu.VMEM((tm, d), F32)],
        compiler_params=_cparams("parallel", "arbitrary"),
        name="ffn",
    )(x, g.reshape(1, d), wg16, wu16, wd16)


def _rmsnorm_kernel(x_ref, g_ref, o_ref):
    x = x_ref[...]
    o_ref[...] = x * lax.rsqrt(jnp.mean(x * x, axis=-1, keepdims=True) + RMS_EPS) * g_ref[...]


def rmsnorm(x, g, tm_pref=512):
    m, d = x.shape
    tm = _row_tile(m, tm_pref)
    return pl.pallas_call(
        _rmsnorm_kernel,
        grid=(m // tm,),
        in_specs=[pl.BlockSpec((tm, d), lambda i: (i, 0)), pl.BlockSpec((1, d), lambda i: (0, 0))],
        out_specs=pl.BlockSpec((tm, d), lambda i: (i, 0)),
        out_shape=jax.ShapeDtypeStruct((m, d), F32),
        compiler_params=_cparams("parallel"),
        name="final_norm",
    )(x, g.reshape(1, d))


PAGES_PER_STEP = 16
STEP_KEYS = PAGES_PER_STEP * PAGE


def _page_specs(block, layer, n_pages, reverse):
    def spec(p):
        def index(b, s, pt):
            j = s * PAGES_PER_STEP + p
            if reverse:
                j = n_pages - 1 - j
            return (layer, pt[b, j], 0, 0)
        return pl.BlockSpec((1, 1) + block, index)
    return [spec(p) for p in range(PAGES_PER_STEP)]


def _per_seq(shape):
    return pl.BlockSpec((1,) + shape, lambda b, s, pt: (b,) + (0,) * len(shape))


def _head_page(ref, h):
    return ref.at[0, 0][pl.ds(h, PAGE, stride=N_HEADS), :].astype(BF16)


def _own_head_rows(per_head, t):
    return jnp.concatenate([per_head[h][h * t:(h + 1) * t] for h in range(N_HEADS)], axis=0)


def _query_index(t, width):
    return jnp.concatenate([_iota((t, width), 0)] * N_HEADS, axis=0)


def _pad_rows(dst, src):
    dst[...] = jnp.zeros(dst.shape, dst.dtype)
    dst[pl.ds(0, src.shape[0]), :] = src


def _per_head_rows(x, t):
    return jnp.concatenate([jnp.broadcast_to(x[h:h + 1], (t, x.shape[1])) for h in range(N_HEADS)], axis=0)


def _per_head_cols(x):
    return jnp.concatenate([x[:, h:h + 1] for h in range(N_HEADS)], axis=0)


def _write_stacked(o_ref, out, t):
    for h in range(N_HEADS):
        o_ref[0, :, h * HEAD_DIM:(h + 1) * HEAD_DIM] = out[h * t:(h + 1) * t].astype(o_ref.dtype)


def _softmax_update(carry, s, pv_fn):
    m, l, acc = carry
    m_new = jnp.maximum(m, jnp.max(s, axis=1, keepdims=True))
    m_safe = jnp.where(m_new == NEG_INF, 0.0, m_new)
    alpha = jnp.exp(m - m_safe)
    p = jnp.exp(s - m_safe)
    l = alpha * l + jnp.sum(p, axis=1, keepdims=True)
    return m_new, l, alpha * acc + pv_fn(p.astype(BF16))


def _softmax_init(rows):
    return (jnp.full((rows, 1), NEG_INF, F32), jnp.zeros((rows, 1), F32), jnp.zeros((rows, HEAD_DIM), F32))


def _new_keys_per_head(q4, pad_ref, t):
    return _own_head_rows([_dot_nt(q4, pad_ref[:, h * HEAD_DIM:(h + 1) * HEAD_DIM].astype(BF16))
                           for h in range(N_HEADS)], t)


def _new_values_per_head(p16, pad_ref, t):
    return _own_head_rows([jnp.dot(p16, pad_ref[:, h * HEAD_DIM:(h + 1) * HEAD_DIM].astype(BF16),
                                   preferred_element_type=F32) for h in range(N_HEADS)], t)


def _page_scores(q4, k_refs, t):
    return _own_head_rows([_dot_nt(q4, jnp.concatenate([_head_page(k_ref, h) for k_ref in k_refs], axis=0))
                           for h in range(N_HEADS)], t)


def _page_values(p16, v_refs, t):
    return _own_head_rows([jnp.dot(p16, jnp.concatenate([_head_page(v_ref, h) for v_ref in v_refs], axis=0),
                                   preferred_element_type=F32) for h in range(N_HEADS)], t)


def _fox_sample_kernel(pt_ref, q_ref, kn_ref, vn_ref, cn_ref, cnrow_ref, *rest, t):
    n = PAGES_PER_STEP
    k_refs, v_refs, lf_refs = rest[0:n], rest[n:2 * n], rest[2 * n:3 * n]
    o_ref, kpad, vpad, m_scr, l_scr, acc_scr, d_scr = rest[3 * n:]
    s_id = pl.program_id(1)
    scale = HEAD_DIM ** -0.5
    rows = N_HEADS * t
    q4 = _stack_heads(q_ref[0]).astype(BF16)
    cn = _per_head_cols(cn_ref[0])

    @pl.when(s_id == 0)
    def _():
        _pad_rows(kpad, kn_ref[0])
        _pad_rows(vpad, vn_ref[0])
        s = _new_keys_per_head(q4, kpad, t) * scale + cn - _per_head_rows(cnrow_ref[0], t)
        col = _iota((rows, CHUNK), 1)
        s = jnp.where((col <= _query_index(t, CHUNK)) & (col < t), s, NEG_INF)
        m, l, acc = _softmax_update(_softmax_init(rows), s, lambda p16: _new_values_per_head(p16, vpad, t))
        m_scr[...], l_scr[...], acc_scr[...] = m, l, acc
        d_scr[...] = jnp.zeros(d_scr.shape, F32)

    suffix = _suffix_matrix()
    later = d_scr[...]
    decays = []
    for p in range(n):
        lf = jnp.concatenate([lf_refs[p][0, 0], jnp.zeros((SUBLANES - N_HEADS, PAGE), F32)], axis=0)
        decays.append(_dot01_right(lf, suffix) + later)
        later = later + jnp.sum(lf, axis=1, keepdims=True)
    decay = jnp.concatenate(decays, axis=1)
    s = _page_scores(q4, k_refs, t) * scale + _per_head_rows(decay[0:N_HEADS], t) + cn
    carry = _softmax_update((m_scr[...], l_scr[...], acc_scr[...]), s, lambda p16: _page_values(p16, v_refs, t))
    m_scr[...], l_scr[...], acc_scr[...] = carry
    d_scr[...] = later

    @pl.when(s_id == pl.num_programs(1) - 1)
    def _():
        _write_stacked(o_ref, carry[2] / carry[1], t)


def _kv_page_block():
    return (PAGE * N_HEADS, HEAD_DIM)


def fox_sample(page_table, layer, q, kn, vn, cn, cnrow, cache_k, cache_v, cache_lft):
    b, t, _ = q.shape
    n_pages = page_table.shape[1]
    rows = N_HEADS * t
    grid_spec = pltpu.PrefetchScalarGridSpec(
        num_scalar_prefetch=1,
        grid=(b, n_pages // PAGES_PER_STEP),
        in_specs=[_per_seq((t, MIX_W)), _per_seq((t, MIX_W)), _per_seq((t, MIX_W)),
                  _per_seq((t, N_HEADS)), _per_seq((N_HEADS, CHUNK))]
                 + _page_specs(_kv_page_block(), layer, n_pages, True)
                 + _page_specs(_kv_page_block(), layer, n_pages, True)
                 + _page_specs((N_HEADS, PAGE), layer, n_pages, True),
        out_specs=_per_seq((t, MIX_W)),
        scratch_shapes=[pltpu.VMEM((CHUNK, MIX_W), F32), pltpu.VMEM((CHUNK, MIX_W), F32),
                        pltpu.VMEM((rows, 1), F32), pltpu.VMEM((rows, 1), F32), pltpu.VMEM((rows, HEAD_DIM), F32),
                        pltpu.VMEM((SUBLANES, 1), F32)])
    return pl.pallas_call(
        functools.partial(_fox_sample_kernel, t=t),
        grid_spec=grid_spec,
        out_shape=jax.ShapeDtypeStruct((b, t, MIX_W), F32),
        compiler_params=_cparams("parallel", "arbitrary"),
        name="fox_sample",
    )(page_table, q, kn, vn, cn, cnrow, *([cache_k] * PAGES_PER_STEP), *([cache_v] * PAGES_PER_STEP),
      *([cache_lft] * PAGES_PER_STEP))


def _sb_sample_kernel(pt_ref, q_ref, kn_ref, vn_ref, *rest, t):
    n = PAGES_PER_STEP
    k_refs, v_refs = rest[0:n], rest[n:2 * n]
    o_ref, kpad, vpad, r_scr, acc_scr = rest[2 * n:]
    s_id = pl.program_id(1)
    scale = HEAD_DIM ** -0.5
    rows = N_HEADS * t
    q4 = _stack_heads(q_ref[0]).astype(BF16)
    suffix = _suffix_matrix()

    @pl.when(s_id == 0)
    def _():
        _pad_rows(kpad, kn_ref[0])
        _pad_rows(vpad, vn_ref[0])
        z = _new_keys_per_head(q4, kpad, t) * scale
        vis = _iota((rows, CHUNK), 1) < _query_index(t, CHUNK)
        log_keep = jnp.where(vis, -_softplus_abs(z), 0.0)
        att = jnp.where(vis, jnp.exp(log_keep + z + _dot01_right(log_keep, suffix, SB_SPLIT)), 0.0)
        acc_scr[...] = _new_values_per_head(att.astype(BF16), vpad, t)
        r_scr[...] = jnp.sum(log_keep, axis=1, keepdims=True)

    later = r_scr[...]
    z = _page_scores(q4, k_refs, t) * scale
    log_keep = -_softplus_abs(z)
    page = lambda x, p: x[:, p * PAGE:(p + 1) * PAGE]
    in_page = _dot01_right(jnp.concatenate([page(log_keep, p) for p in range(n)], axis=0), suffix, SB_SPLIT)
    pieces = []
    for p in range(n):
        suf = in_page[p * rows:(p + 1) * rows]
        pieces.append(jnp.exp(page(log_keep, p) + page(z, p) + suf + later).astype(BF16))
        later = later + suf[:, 0:1] + page(log_keep, p)[:, 0:1]
    acc = acc_scr[...] + _page_values(jnp.concatenate(pieces, axis=1), v_refs, t)
    r_scr[...], acc_scr[...] = later, acc

    @pl.when(s_id == pl.num_programs(1) - 1)
    def _():
        _write_stacked(o_ref, acc, t)


def sb_sample(page_table, layer, q, kn, vn, cache_k, cache_v):
    b, t, _ = q.shape
    n_pages = page_table.shape[1]
    rows = N_HEADS * t
    grid_spec = pltpu.PrefetchScalarGridSpec(
        num_scalar_prefetch=1,
        grid=(b, n_pages // PAGES_PER_STEP),
        in_specs=[_per_seq((t, MIX_W))] * 3
                 + _page_specs(_kv_page_block(), layer, n_pages, True)
                 + _page_specs(_kv_page_block(), layer, n_pages, True),
        out_specs=_per_seq((t, MIX_W)),
        scratch_shapes=[pltpu.VMEM((CHUNK, MIX_W), F32), pltpu.VMEM((CHUNK, MIX_W), F32),
                        pltpu.VMEM((rows, 1), F32), pltpu.VMEM((rows, HEAD_DIM), F32)])
    return pl.pallas_call(
        functools.partial(_sb_sample_kernel, t=t),
        grid_spec=grid_spec,
        out_shape=jax.ShapeDtypeStruct((b, t, MIX_W), F32),
        compiler_params=_cparams("parallel", "arbitrary"),
        name="sb_sample",
    )(page_table, q, kn, vn, *([cache_k] * PAGES_PER_STEP), *([cache_v] * PAGES_PER_STEP))


def _weighted_relu_sum(g, wi, t):
    score = None
    for h in range(IDX_HEADS):
        term = wi[:, h:h + 1] * jnp.maximum(g[h * t:(h + 1) * t], 0.0)
        score = term if score is None else score + term
    return score * ((IDX_DIM * IDX_HEADS) ** -0.5)


def _dsa_scores_kernel(pt_ref, qi_ref, misc_ref, *rest, t):
    n = PAGES_PER_STEP
    idx_refs = rest[0:n]
    past_ref, new_ref, kpad = rest[n:]
    qi = qi_ref[0]
    qi_stack = jnp.concatenate([qi[:, h * IDX_DIM:(h + 1) * IDX_DIM] for h in range(IDX_HEADS)], axis=0).astype(BF16)
    wi = misc_ref[0, :, MISC_WI:MISC_WI + IDX_HEADS]

    @pl.when(pl.program_id(1) == 0)
    def _():
        _pad_rows(kpad, misc_ref[0])
        g = _dot_nt(qi_stack, kpad[:, MISC_KI:MISC_KI + IDX_DIM].astype(BF16))
        new_ref[0] = _weighted_relu_sum(g, wi, t)

    for p in range(n):
        g = jnp.dot(qi_stack, idx_refs[p][0, 0].astype(BF16), preferred_element_type=F32)
        past_ref[0, :, p * PAGE:(p + 1) * PAGE] = _weighted_relu_sum(g, wi, t)


def dsa_sample_scores(page_table, layer, qi, misc, cache_idx_t):
    b, t, _ = qi.shape
    n_pages = page_table.shape[1]
    grid_spec = pltpu.PrefetchScalarGridSpec(
        num_scalar_prefetch=1,
        grid=(b, n_pages // PAGES_PER_STEP),
        in_specs=[_per_seq((t, MIX_W)), _per_seq((t, LANES))] + _page_specs((IDX_DIM, PAGE), layer, n_pages, False),
        out_specs=[pl.BlockSpec((1, t, STEP_KEYS), lambda i, s, pt: (i, 0, s)), _per_seq((t, CHUNK))],
        scratch_shapes=[pltpu.VMEM((CHUNK, LANES), F32)])
    return pl.pallas_call(
        functools.partial(_dsa_scores_kernel, t=t),
        grid_spec=grid_spec,
        out_shape=[jax.ShapeDtypeStruct((b, t, n_pages * PAGE), F32), jax.ShapeDtypeStruct((b, t, CHUNK), F32)],
        compiler_params=_cparams("parallel", "arbitrary"),
        name="dsa_sample_scores",
    )(page_table, qi, misc, *([cache_idx_t] * PAGES_PER_STEP))


SELECT_ROWS = 64


def _dsa_select_kernel(score_ref, bias_ref, key_scr, *, t, n_past, topk):
    rows, width = score_ref.shape
    qidx = jnp.concatenate([_iota((t, 1), 0)] * (rows // t), axis=0)

    def vis_fn(c0, w):
        col = c0 + _iota((rows, w), 1)
        return (col < n_past) | ((col - n_past <= qidx) & (col - n_past < t))

    key_scr[...] = jnp.where(vis_fn(0, width), _order_key(score_ref[...]), INT_MIN)
    _select_topk(key_scr, bias_ref, vis_fn, rows, width, topk)


def dsa_sample_select(scores, t, n_past, topk):
    m, width = scores.shape
    rows = _row_tile(m, SELECT_ROWS)
    assert rows % t == 0
    spec = pl.BlockSpec((rows, width), lambda i: (i, 0))
    return pl.pallas_call(
        functools.partial(_dsa_select_kernel, t=t, n_past=n_past, topk=topk),
        grid=(m // rows,),
        in_specs=[spec],
        out_specs=spec,
        out_shape=jax.ShapeDtypeStruct(scores.shape, F32),
        scratch_shapes=[pltpu.VMEM((rows, width), I32)],
        compiler_params=_cparams("parallel"),
        name="dsa_sample_select",
    )(scores)


def _dsa_attend_kernel(pt_ref, q_ref, kn_ref, vn_ref, bias_new_ref, bias_past_ref, *rest, t):
    n = PAGES_PER_STEP
    k_refs, v_refs = rest[0:n], rest[n:2 * n]
    o_ref, kpad, vpad, m_scr, l_scr, acc_scr = rest[2 * n:]
    s_id = pl.program_id(1)
    scale = HEAD_DIM ** -0.5
    rows = N_HEADS * t
    q4 = _stack_heads(q_ref[0]).astype(BF16)

    @pl.when(s_id == 0)
    def _():
        _pad_rows(kpad, kn_ref[0])
        _pad_rows(vpad, vn_ref[0])
        s = _dot_nt(q4, kpad[...].astype(BF16)) * scale + jnp.concatenate([bias_new_ref[0]] * N_HEADS, axis=0)
        m, l, acc = _softmax_update(
            _softmax_init(rows), s,
            lambda p16: jnp.dot(p16, vpad[...].astype(BF16), preferred_element_type=F32))
        m_scr[...], l_scr[...], acc_scr[...] = m, l, acc

    step_rows = lambda refs: jnp.concatenate([r[0, 0].astype(BF16) for r in refs], axis=0)
    s = _dot_nt(q4, step_rows(k_refs)) * scale + jnp.concatenate([bias_past_ref[0]] * N_HEADS, axis=0)
    carry = _softmax_update((m_scr[...], l_scr[...], acc_scr[...]), s,
                            lambda p16: jnp.dot(p16, step_rows(v_refs), preferred_element_type=F32))
    m_scr[...], l_scr[...], acc_scr[...] = carry

    @pl.when(s_id == pl.num_programs(1) - 1)
    def _():
        _write_stacked(o_ref, carry[2] / carry[1], t)


def dsa_sample_attend(page_table, layer, q, kn, vn, bias, cache_k, cache_v):
    b, t, _ = q.shape
    n_pages = page_table.shape[1]
    rows = N_HEADS * t
    grid_spec = pltpu.PrefetchScalarGridSpec(
        num_scalar_prefetch=1,
        grid=(b, n_pages // PAGES_PER_STEP),
        in_specs=[_per_seq((t, MIX_W)), _per_seq((t, HEAD_DIM)), _per_seq((t, HEAD_DIM)),
                  pl.BlockSpec((1, t, CHUNK), lambda i, s, pt: (i, 0, n_pages)),
                  pl.BlockSpec((1, t, STEP_KEYS), lambda i, s, pt: (i, 0, s))]
                 + _page_specs((PAGE, HEAD_DIM), layer, n_pages, False)
                 + _page_specs((PAGE, HEAD_DIM), layer, n_pages, False),
        out_specs=_per_seq((t, MIX_W)),
        scratch_shapes=[pltpu.VMEM((CHUNK, HEAD_DIM), F32), pltpu.VMEM((CHUNK, HEAD_DIM), F32),
                        pltpu.VMEM((rows, 1), F32), pltpu.VMEM((rows, 1), F32), pltpu.VMEM((rows, HEAD_DIM), F32)])
    return pl.pallas_call(
        functools.partial(_dsa_attend_kernel, t=t),
        grid_spec=grid_spec,
        out_shape=jax.ShapeDtypeStruct((b, t, MIX_W), F32),
        compiler_params=_cparams("parallel", "arbitrary"),
        name="dsa_sample_attend",
    )(page_table, q, kn, vn, bias, bias, *([cache_k] * PAGES_PER_STEP), *([cache_v] * PAGES_PER_STEP))


def _pad_w_in(w):
    widths = (512, 512, 512, 4, 512, 512, 512, 512, 128, 128, 512, 64, 8, 512)
    names = ("qa", "ka", "va", "fa", "qb", "kb", "vb", "qc", "kc", "vc", "qi", "ki", "wi", "xd")
    seg, off = {}, 0
    for name, width in zip(names, widths):
        seg[name] = w[:, :, off:off + width]
        off += width
    order = ("qa", "ka", "va", "qb", "kb", "vb", "qc", "qi", "xd", "kc", "vc", "ki", "fa", "wi")
    cols = jnp.concatenate([seg[n] for n in order], axis=2)
    return jnp.pad(cols, ((0, 0), (0, 0), (0, N_PROJ - cols.shape[2]))).astype(BF16)


def _cols(p, off, width):
    return p[..., off:off + width]


def _tail_cols(tail32, off, width):
    return _cols(tail32, off - OFF_XD, width)


def _small_rows(tail32, lf):
    return (_cols(lf, MISC_FA, N_HEADS), _tail_cols(tail32, OFF_KC, HEAD_DIM),
            _tail_cols(tail32, OFF_VC, HEAD_DIM), _tail_cols(tail32, OFF_MISC + MISC_KI, IDX_DIM))


def _assemble_rows(head_rows, small_rows, b, t):
    ka, va, kb, vb = (r.reshape(r.shape[0], b, t, N_HEADS, HEAD_DIM) for r in head_rows)
    logf, kc, vc, ki, conv, h_last = (jnp.stack(list(r)) for r in zip(*small_rows))
    return (ka, va, logf, kb, vb, kc, vc, ki, conv, h_last)


def _dense_tail(x2d, h16, ys, lw):
    merged = merge(h16, ys, lw["w_gate"], lw["w_branch"])
    x2d = matmul_residual(merged, lw["w_out"], x2d)
    return ffn(x2d, lw["g_ffn"], lw["w_fg"], lw["w_fu"], lw["w_fd"])


def _forget_bias_row(b_f):
    return jnp.zeros((1, LANES), F32).at[0, MISC_FA:MISC_FA + N_HEADS].set(b_f)


def _conv_tail(conv_state, tail32):
    xd = _tail_cols(tail32, OFF_XD, MIX_W)
    return jnp.concatenate([conv_state, xd], axis=1)[:, -(CONV_W - 1):]


def _prompt_layer(x2d, b, t, lw, topk, layer, depth, prev_head_rows):
    tail32, p16, h16, *head_rows = norm_proj(x2d, lw["g_mix"], lw["w_in_t"], layer, depth, prev_head_rows)
    tail32, p16 = tail32.reshape(b, t, N_TAIL), p16.reshape(b, t, N_PROJ)
    lf, fc = logf_cumsum(_tail_cols(tail32, OFF_MISC, LANES), _forget_bias_row(lw["b_f"]))
    tp = pl.cdiv(t, CHUNK) * CHUNK
    frow = jnp.pad(_cols(fc, MISC_FA, N_HEADS), ((0, 0), (0, tp - t), (0, 0))).transpose(0, 2, 1)
    conv0 = jnp.zeros((b, CONV_W - 1, MIX_W), F32)
    ya = fox_prompt(p16, frow.reshape(b, N_HEADS, 1, tp))
    yb = sb_prompt(p16)
    yc = dsa_prompt(tail32, p16, topk)
    yd, h_last = rglru(tail32, conv0, jnp.zeros((b, MIX_W), F32), lw, BF16)
    ys = [y.reshape(b * t, MIX_W) for y in (ya, yb, yc, yd)]
    rows = _small_rows(tail32, lf) + (_conv_tail(conv0, tail32), h_last.reshape(b, MIX_W))
    return _dense_tail(x2d, h16, ys, lw), rows, head_rows


def _sample_layer(x2d, b, t, lw, topk, layer, depth, prev_head_rows, page_table, caches, conv_state, h0):
    ca_k, ca_v, ca_lft, cb_k, cb_v, cc_k, cc_v, cc_idx_t = caches
    n_past = page_table.shape[1] * PAGE
    tail32, p16, h16, *head_rows = norm_proj(x2d, lw["g_mix"], lw["w_in_t"], layer, depth, prev_head_rows)
    tail32 = tail32.reshape(b, t, N_TAIL)
    seg = lambda off, width: _cols(p16, off, width).astype(F32).reshape(b, t, width)
    misc = _tail_cols(tail32, OFF_MISC, LANES)
    lf, fc = logf_cumsum(misc, _forget_bias_row(lw["b_f"]))
    cn = _cols(fc, MISC_FA, N_HEADS)
    cnrow = jnp.pad(cn.transpose(0, 2, 1), ((0, 0), (0, 0), (0, CHUNK - t)))
    ya = fox_sample(page_table, layer, seg(OFF_QA, MIX_W), seg(OFF_KA, MIX_W), seg(OFF_VA, MIX_W),
                    cn, cnrow, ca_k, ca_v, ca_lft)
    yb = sb_sample(page_table, layer, seg(OFF_QB, MIX_W), seg(OFF_KB, MIX_W), seg(OFF_VB, MIX_W), cb_k, cb_v)
    score_past, score_new = dsa_sample_scores(page_table, layer, seg(OFF_QI, MIX_W), misc, cc_idx_t)
    scores = jnp.concatenate([score_past, score_new], axis=-1).reshape(b * t, n_past + CHUNK)
    bias = dsa_sample_select(scores, t, n_past, topk).reshape(b, t, n_past + CHUNK)
    yc = dsa_sample_attend(page_table, layer, seg(OFF_QC, MIX_W), seg(OFF_KC, HEAD_DIM), seg(OFF_VC, HEAD_DIM),
                           bias, cc_k, cc_v)
    yd, h_last = rglru(tail32, conv_state, h0, lw, F32)
    ys = [y.reshape(b * t, MIX_W).astype(BF16) for y in (ya, yb, yc, yd)]
    rows = _small_rows(tail32, lf) + (_conv_tail(conv_state, tail32), h_last.reshape(b, MIX_W))
    return _dense_tail(x2d, h16, ys, lw), rows, head_rows


def kernel(x_prompt, x_sample, cache_a_k, cache_a_v, cache_a_logf, cache_b_k, cache_b_v, cache_c_k, cache_c_v, cache_c_idx_k, state_d_conv, state_d_h, page_table, meta_tokens, w_in, b_forget, conv_w, conv_b, w_rg_a, b_rg_a, w_rg_x, b_rg_x, rg_lambda, w_gate, w_branch, w_out, norm_mix, norm_ffn, w_ffn_gate, w_ffn_up, w_ffn_down, norm_final):
    depth = w_in.shape[0]
    n_p, seq_p, _ = x_prompt.shape
    n_dec, t_dec, _ = x_sample.shape
    n_pool = cache_a_k.shape[1]
    past_len = page_table.shape[1] * PAGE
    topk_p = min(TOPK_MAX, seq_p // 4)
    topk_s = min(TOPK_MAX, (past_len + t_dec) // 4)

    rows_view = lambda c: c.reshape(depth, n_pool, PAGE * N_HEADS, HEAD_DIM)
    caches = (rows_view(cache_a_k), rows_view(cache_a_v), cache_a_logf.transpose(0, 1, 3, 2),
              rows_view(cache_b_k), rows_view(cache_b_v), cache_c_k, cache_c_v, cache_c_idx_k.transpose(0, 1, 3, 2))

    xp = jnp.concatenate([jnp.broadcast_to(meta_tokens[None], (n_p, N_META, D_MODEL)), x_prompt], axis=1)
    t_p = seq_p + N_META
    xp = xp.reshape(n_p * t_p, D_MODEL)
    xs = x_sample.reshape(n_dec * t_dec, D_MODEL)

    w_in_p = _pad_w_in(w_in)
    prompt_rows, sample_rows, prompt_heads, sample_heads = [], [], None, None
    for l in range(depth):
        lw = dict(w_in_t=w_in_p[l].T, b_f=b_forget[l], conv_w=conv_w[l], conv_b=conv_b[l],
                  w_a=_block_diag(w_rg_a[l]).astype(BF16), b_a=b_rg_a[l],
                  w_x=_block_diag(w_rg_x[l]).astype(BF16), b_x=b_rg_x[l], lam=rg_lambda[l],
                  w_gate=w_gate[l].astype(BF16), w_branch=w_branch[l].astype(BF16), w_out=w_out[l].astype(BF16),
                  g_mix=norm_mix[l], g_ffn=norm_ffn[l], w_fg=w_ffn_gate[l].astype(BF16),
                  w_fu=w_ffn_up[l].astype(BF16), w_fd=w_ffn_down[l].astype(BF16))
        xp, rows, prompt_heads = _prompt_layer(xp, n_p, t_p, lw, topk_p, l, depth, prompt_heads)
        prompt_rows.append(rows)
        xs, rows, sample_heads = _sample_layer(xs, n_dec, t_dec, lw, topk_s, l, depth, sample_heads, page_table, caches,
                                               state_d_conv[l], state_d_h[l])
        sample_rows.append(rows)

    y_prompt = rmsnorm(xp, norm_final).reshape(n_p, t_p, D_MODEL)[:, N_META:]
    y_sample = rmsnorm(xs, norm_final).reshape(n_dec, t_dec, D_MODEL)
    return ((y_prompt, y_sample) + _assemble_rows(prompt_heads, prompt_rows, n_p, t_p)
            + _assemble_rows(sample_heads, sample_rows, n_dec, t_dec))
```

```python
import functools

import jax
import jax.numpy as jnp
from jax import lax
from jax.experimental import pallas as pl
from jax.experimental.pallas import tpu as pltpu

F32 = jnp.float32
BF16 = jnp.bfloat16
I32 = jnp.int32

D_MODEL = 2048
N_META = 16
HEAD_DIM = 128
N_HEADS = 4
MIX_W = 512
IDX_HEADS = 8
IDX_DIM = 64
TOPK_MAX = 256
RG_C = 8.0
CONV_W = 4
RMS_EPS = 1e-6
PAGE = 128

LANES = 128
SUBLANES = 8
CHUNK = 128
BAND_TILES = 2
HEADS_PER_ITER = 4
VMEM_LIMIT = 56 * 1024 * 1024

OFF_QA, OFF_KA, OFF_VA = 0, 512, 1024
OFF_QB, OFF_KB, OFF_VB = 1536, 2048, 2560
OFF_QC, OFF_QI, OFF_XD = 3072, 3584, 4096
OFF_KC, OFF_VC, OFF_MISC = 4608, 4736, 4864
MISC_KI, MISC_FA, MISC_WI = 0, 64, 68
N_PROJ = 5120

NEG_INF = float("-inf")
INT_MIN = -2 ** 31


def _cparams(*sem):
    return pltpu.CompilerParams(dimension_semantics=sem, vmem_limit_bytes=VMEM_LIMIT)


def _split_bf16(x, pieces):
    out, rest = [], x
    for n in range(pieces):
        part = rest.astype(BF16)
        out.append(part)
        if n + 1 < pieces:
            rest = rest - part.astype(F32)
    return out


def _dot01_left(m01, x, pieces=3):
    return sum(jnp.dot(m01, p, preferred_element_type=F32) for p in _split_bf16(x, pieces))


def _dot01_right(x, m01, pieces=3):
    return sum(jnp.dot(p, m01, preferred_element_type=F32) for p in _split_bf16(x, pieces))


def _dot_nt(a, b):
    return lax.dot_general(a, b, (((1,), (1,)), ((), ())), preferred_element_type=F32)


def _softplus(x, log1p=jnp.log1p):
    return jnp.maximum(x, 0.0) + log1p(jnp.exp(-jnp.abs(x)))


def _softplus_abs(x):
    return _softplus(x, lambda e: jnp.log(1.0 + e))


SB_SPLIT = 2


def _iota(shape, dim):
    return lax.broadcasted_iota(I32, shape, dim)


def _row_tile(m, pref):
    for t in range(min(pref, m), 7, -1):
        if m % t == 0 and t % SUBLANES == 0:
            return t
    raise ValueError(f"no row tile for {m}")


PROJ_TN = 512
ROW_TILES = tuple(off // PROJ_TN for off in (OFF_KA, OFF_VA, OFF_KB, OFF_VB))
TAIL_TILE0 = OFF_XD // PROJ_TN
N_TAIL = N_PROJ - OFF_XD


def _norm_proj_kernel(x_ref, g_ref, wt_ref, *rest):
    tail_ref, o16_ref, h16_ref, ka_ref, va_ref, kb_ref, vb_ref, h_scr = rest[-8:]
    j = pl.program_id(1)
    tm = x_ref.shape[0]

    @pl.when(j == 0)
    def _():
        x = x_ref[...]
        y = x * lax.rsqrt(jnp.mean(x * x, axis=-1, keepdims=True) + RMS_EPS)
        h = (y * g_ref[...]).astype(BF16)
        h_scr[...] = h
        h16_ref[...] = h

    acc = _dot_nt(h_scr[...], wt_ref[...])
    o16_ref[...] = acc.astype(BF16)

    @pl.when(j >= TAIL_TILE0)
    def _():
        tail_ref[...] = acc

    for tile, rows_ref in zip(ROW_TILES, (ka_ref, va_ref, kb_ref, vb_ref)):
        @pl.when(j == tile)
        def _(rows_ref=rows_ref):
            for h in range(N_HEADS):
                rows_ref.at[0][pl.ds(h, tm, stride=N_HEADS), :] = acc[:, h * HEAD_DIM:(h + 1) * HEAD_DIM]


def norm_proj(x, g, wt16, layer, depth, prev_rows, tm_pref=1032):
    m, d = x.shape
    n = wt16.shape[0]
    tm = _row_tile(m, tm_pref)
    rows_spec = pl.BlockSpec((1, tm * N_HEADS, HEAD_DIM), lambda i, j: (layer, i, 0))
    rows_shape = jax.ShapeDtypeStruct((depth, m * N_HEADS, HEAD_DIM), F32)
    prev = list(prev_rows) if prev_rows is not None else []
    return pl.pallas_call(
        _norm_proj_kernel,
        grid=(m // tm, n // PROJ_TN),
        in_specs=[pl.BlockSpec((tm, d), lambda i, j: (i, 0), pipeline_mode=pl.Buffered(1)),
                  pl.BlockSpec((1, d), lambda i, j: (0, 0)),
                  pl.BlockSpec((PROJ_TN, d), lambda i, j: (j, 0))] + [pl.BlockSpec(memory_space=pl.ANY)] * len(prev),
        out_specs=[pl.BlockSpec((tm, PROJ_TN), lambda i, j: (i, jnp.maximum(j - TAIL_TILE0, 0))),
                   pl.BlockSpec((tm, PROJ_TN), lambda i, j: (i, j)),
                   pl.BlockSpec((tm, d), lambda i, j: (i, 0)),
                   rows_spec, rows_spec, rows_spec, rows_spec],
        out_shape=[jax.ShapeDtypeStruct((m, N_TAIL), F32),
                   jax.ShapeDtypeStruct((m, n), BF16),
                   jax.ShapeDtypeStruct((m, d), BF16),
                   rows_shape, rows_shape, rows_shape, rows_shape],
        input_output_aliases={3 + k: 3 + k for k in range(len(prev))},
        scratch_shapes=[pltpu.VMEM((tm, d), BF16)],
        compiler_params=_cparams("parallel", "arbitrary"),
        name="norm_proj",
    )(x, g.reshape(1, d), wt16, *prev)


def _logf_kernel(misc_ref, bias_ref, logf_ref, fcum_ref, pad_scr, *, t, n_chunks):
    lane = _iota((1, LANES), 1)
    live = (lane >= MISC_FA) & (lane < MISC_FA + N_HEADS)
    logf = jnp.where(live, jax.nn.log_sigmoid(misc_ref[0] + bias_ref[...]), 0.0)
    logf_ref[0] = logf
    pad_scr[pl.ds(0, t), :] = logf
    if n_chunks * CHUNK > t:
        pad_scr[pl.ds(t, n_chunks * CHUNK - t), :] = jnp.zeros((n_chunks * CHUNK - t, LANES), F32)
    tril = (_iota((CHUNK, CHUNK), 1) <= _iota((CHUNK, CHUNK), 0)).astype(BF16)
    carry = jnp.zeros((1, LANES), F32)
    for c in range(n_chunks):
        f = _dot01_left(tril, pad_scr[pl.ds(c * CHUNK, CHUNK), :]) + carry
        rows = min(CHUNK, t - c * CHUNK)
        fcum_ref[0, pl.ds(c * CHUNK, rows), :] = f[:rows]
        carry = f[CHUNK - 1:CHUNK, :]


def logf_cumsum(misc, bias_row):
    b, t, _ = misc.shape
    n_chunks = pl.cdiv(t, CHUNK)
    return pl.pallas_call(
        functools.partial(_logf_kernel, t=t, n_chunks=n_chunks),
        grid=(b,),
        in_specs=[pl.BlockSpec((1, t, LANES), lambda i: (i, 0, 0)),
                  pl.BlockSpec((1, LANES), lambda i: (0, 0))],
        out_specs=[pl.BlockSpec((1, t, LANES), lambda i: (i, 0, 0)),
                   pl.BlockSpec((1, t, LANES), lambda i: (i, 0, 0))],
        out_shape=[jax.ShapeDtypeStruct((b, t, LANES), F32)] * 2,
        scratch_shapes=[pltpu.VMEM((n_chunks * CHUNK, LANES), F32)],
        compiler_params=_cparams("parallel"),
        name="logf_cumsum",
    )(misc, bias_row)


def _pad_copy(dst, src, t):
    rows = dst.shape[0]
    dst[pl.ds(0, t), :] = src
    if rows > t:
        dst[pl.ds(t, rows - t), :] = jnp.zeros((rows - t, dst.shape[1]), dst.dtype)


def _stage_heads(dst, src_ref, t):
    for h in range(N_HEADS):
        _pad_copy(dst.at[h], src_ref[0, :, h * HEAD_DIM:(h + 1) * HEAD_DIM], t)


def _unstage_heads(o_ref, src, t):
    for h in range(N_HEADS):
        o_ref[0, :, h * HEAD_DIM:(h + 1) * HEAD_DIM] = src[h, 0:t, :]


def _bands(n_tiles):
    return [(lo, min(lo + BAND_TILES, n_tiles)) for lo in range(0, n_tiles, BAND_TILES)]


def _for_heads_and_tiles(n_tiles, tile):
    for lo, hi in _bands(n_tiles):
        def per_head_pair(hp, _, lo=lo, hi=hi):
            def per_tile(i, _):
                for s in range(HEADS_PER_ITER):
                    tile(hp * HEADS_PER_ITER + s, i, hi * CHUNK)
                return 0
            return lax.fori_loop(lo, hi, per_tile, 0)
        lax.fori_loop(0, N_HEADS // HEADS_PER_ITER, per_head_pair, 0)


def _fox_prompt_kernel(q_ref, k_ref, v_ref, frow_ref, o_ref, q_scr, k_scr, v_scr, o_scr, *, t):
    scale = HEAD_DIM ** -0.5
    _stage_heads(q_scr, q_ref, t)
    _stage_heads(k_scr, k_ref, t)
    _stage_heads(v_scr, v_ref, t)

    def tile(h, i, width):
        row0 = pl.multiple_of(i * CHUNK, CHUNK)
        q = q_scr[h, pl.ds(row0, CHUNK), :]
        s = _dot_nt(q, k_scr[h, 0:width, :]) * scale - frow_ref[0, h, :, 0:width]
        vis = _iota((CHUNK, width), 1) <= row0 + _iota((CHUNK, width), 0)
        s = jnp.where(vis, s, NEG_INF)
        p = jnp.exp(s - jnp.max(s, axis=1, keepdims=True))
        l = jnp.sum(p, axis=1, keepdims=True)
        o = jnp.dot(p.astype(BF16), v_scr[h, 0:width, :], preferred_element_type=F32) / l
        o_scr[h, pl.ds(row0, CHUNK), :] = o.astype(o_scr.dtype)

    _for_heads_and_tiles(k_scr.shape[1] // CHUNK, tile)
    _unstage_heads(o_ref, o_scr, t)


def _head_scratch(tp, n):
    return [pltpu.VMEM((N_HEADS, tp, HEAD_DIM), BF16)] * n


def fox_prompt(proj16, frow):
    b, t, _ = proj16.shape
    tp = frow.shape[-1]
    blk = lambda off: pl.BlockSpec((1, t, MIX_W), lambda i, o=off // MIX_W: (i, 0, o))
    return pl.pallas_call(
        functools.partial(_fox_prompt_kernel, t=t),
        grid=(b,),
        in_specs=[blk(OFF_QA), blk(OFF_KA), blk(OFF_VA),
                  pl.BlockSpec((1, N_HEADS, 1, tp), lambda i: (i, 0, 0, 0))],
        out_specs=pl.BlockSpec((1, t, MIX_W), lambda i: (i, 0, 0)),
        out_shape=jax.ShapeDtypeStruct((b, t, MIX_W), BF16),
        scratch_shapes=_head_scratch(tp, 4),
        compiler_params=_cparams("parallel"),
        name="fox_prompt",
    )(proj16, proj16, proj16, frow)


def _suffix_matrix():
    return (_iota((CHUNK, CHUNK), 0) > _iota((CHUNK, CHUNK), 1)).astype(BF16)


def _sb_prompt_kernel(q_ref, k_ref, v_ref, o_ref, q_scr, k_scr, v_scr, o_scr, *, t):
    scale = HEAD_DIM ** -0.5
    _stage_heads(q_scr, q_ref, t)
    _stage_heads(k_scr, k_ref, t)
    _stage_heads(v_scr, v_ref, t)
    suffix = _suffix_matrix()

    def tile(h, i, width):
        row0 = pl.multiple_of(i * CHUNK, CHUNK)
        q = q_scr[h, pl.ds(row0, CHUNK), :]
        z = _dot_nt(q, k_scr[h, 0:width, :]) * scale
        vis = _iota((CHUNK, width), 1) < row0 + _iota((CHUNK, width), 0)
        log_keep = jnp.where(vis, -_softplus_abs(z), 0.0)
        later_chunks = jnp.zeros((CHUNK, 1), F32)
        pieces = [None] * (width // CHUNK)
        for c in reversed(range(width // CHUNK)):
            cs = slice(c * CHUNK, (c + 1) * CHUNK)
            later = _dot01_right(log_keep[:, cs], suffix, SB_SPLIT) + later_chunks
            att = jnp.where(vis[:, cs], jnp.exp(log_keep[:, cs] + z[:, cs] + later), 0.0)
            pieces[c] = att.astype(BF16)
            later_chunks = later_chunks + jnp.sum(log_keep[:, cs], axis=1, keepdims=True)
        att = jnp.concatenate(pieces, axis=1)
        o = jnp.dot(att, v_scr[h, 0:width, :], preferred_element_type=F32)
        o_scr[h, pl.ds(row0, CHUNK), :] = o.astype(o_scr.dtype)

    _for_heads_and_tiles(k_scr.shape[1] // CHUNK, tile)
    _unstage_heads(o_ref, o_scr, t)


def sb_prompt(proj16):
    b, t, _ = proj16.shape
    tp = pl.cdiv(t, CHUNK) * CHUNK
    blk = lambda off: pl.BlockSpec((1, t, MIX_W), lambda i, o=off // MIX_W: (i, 0, o))
    return pl.pallas_call(
        functools.partial(_sb_prompt_kernel, t=t),
        grid=(b,),
        in_specs=[blk(OFF_QB), blk(OFF_KB), blk(OFF_VB)],
        out_specs=pl.BlockSpec((1, t, MIX_W), lambda i: (i, 0, 0)),
        out_shape=jax.ShapeDtypeStruct((b, t, MIX_W), BF16),
        scratch_shapes=_head_scratch(tp, 4),
        compiler_params=_cparams("parallel"),
        name="sb_prompt",
    )(proj16, proj16, proj16)


def _order_key(score):
    score = jnp.where(score == 0.0, 0.0, score)
    bits = lax.bitcast_convert_type(score, I32)
    return bits ^ ((bits >> 31) & 0x7FFFFFFF)


def _select_topk(key_ref, bias_ref, vis_fn, rows, width, topk):
    kf = float(topk)

    def count(pred):
        return jnp.sum(jnp.where(pred(key_ref[0:rows, 0:width]), 1.0, 0.0), axis=1, keepdims=True)

    thr0 = jnp.where(count(lambda k: k >= 0) >= kf, 0, INT_MIN).astype(I32)

    def bit_step(b, thr):
        cand = thr | (jnp.int32(1) << (30 - b))
        return jnp.where(count(lambda k: k >= cand) >= kf, cand, thr)

    thr = lax.fori_loop(0, 31, bit_step, thr0)
    need = kf - count(lambda k: k > thr)
    keys = key_ref[0:rows, 0:width]
    vis = vis_fn(0, width)
    n_tie = jnp.sum(jnp.where((keys == thr) & vis, 1.0, 0.0), axis=1, keepdims=True)
    bias_ref[0:rows, 0:width] = jnp.where((keys >= thr) & vis, 0.0, NEG_INF)

    @pl.when(jnp.max(n_tie - need) > 0.0)
    def _():
        prefix = (_iota((CHUNK, CHUNK), 0) <= _iota((CHUNK, CHUNK), 1)).astype(BF16)
        seen = jnp.zeros((rows, 1), F32)
        for c in range(width // CHUNK):
            k = key_ref[0:rows, c * CHUNK:(c + 1) * CHUNK]
            v = vis_fn(c * CHUNK, CHUNK)
            tie = jnp.where((k == thr) & v, 1.0, 0.0)
            rank = jnp.dot(tie.astype(BF16), prefix, preferred_element_type=F32) + seen
            sel = ((k > thr) & v) | ((tie > 0.0) & (rank <= need))
            bias_ref[0:rows, c * CHUNK:(c + 1) * CHUNK] = jnp.where(sel, 0.0, NEG_INF)
            seen = seen + jnp.sum(tie, axis=1, keepdims=True)


def _stack_heads(q):
    return jnp.concatenate([q[:, h * HEAD_DIM:(h + 1) * HEAD_DIM] for h in range(N_HEADS)], axis=0)


def _dsa_prompt_kernel(q_ref, qi_ref, kc_ref, vc_ref, misc32_ref, misc16_ref, o_ref,
                       kc_scr, vc_scr, ki_scr, key_scr, bias_scr, *, t, topk):
    n_full, tail = t // CHUNK, t % CHUNK
    scale = HEAD_DIM ** -0.5
    idx_scale = (IDX_DIM * IDX_HEADS) ** -0.5
    _pad_copy(kc_scr, kc_ref[0], t)
    _pad_copy(vc_scr, vc_ref[0], t)
    _pad_copy(ki_scr, misc16_ref[0], t)

    def tile(row0, tq, width):
        def vis_fn(c0, w):
            return (c0 + _iota((tq, w), 1)) <= (row0 + _iota((tq, w), 0))

        qi = qi_ref[0, pl.ds(row0, tq), :]
        wi = misc32_ref[0, pl.ds(row0, tq), MISC_WI:MISC_WI + IDX_HEADS]
        ki = ki_scr[0:width, MISC_KI:MISC_KI + IDX_DIM]
        score = None
        for h in range(IDX_HEADS):
            term = wi[:, h:h + 1] * jnp.maximum(_dot_nt(qi[:, h * IDX_DIM:(h + 1) * IDX_DIM], ki), 0.0)
            score = term if score is None else score + term
        key_scr[0:tq, 0:width] = jnp.where(vis_fn(0, width), _order_key(score * idx_scale), INT_MIN)
        _select_topk(key_scr, bias_scr, vis_fn, tq, width, topk)

        q4 = _stack_heads(q_ref[0, pl.ds(row0, tq), :])
        bias = bias_scr[0:tq, 0:width]
        s = _dot_nt(q4, kc_scr[0:width, :]) * scale + jnp.concatenate([bias] * N_HEADS, axis=0)
        p = jnp.exp(s - jnp.max(s, axis=1, keepdims=True))
        l = jnp.sum(p, axis=1, keepdims=True)
        out = jnp.dot(p.astype(BF16), vc_scr[0:width, :], preferred_element_type=F32) / l
        for h in range(N_HEADS):
            o_ref[0, pl.ds(row0, tq), h * HEAD_DIM:(h + 1) * HEAD_DIM] = out[h * tq:(h + 1) * tq].astype(o_ref.dtype)

    for lo, hi in _bands(n_full):
        tile(lo * CHUNK, (hi - lo) * CHUNK, hi * CHUNK)
    if tail:
        tile(n_full * CHUNK, tail, (n_full + 1) * CHUNK)


def dsa_prompt(tail32, proj16, topk):
    b, t, _ = proj16.shape
    tp = pl.cdiv(t, CHUNK) * CHUNK
    wide = lambda off: pl.BlockSpec((1, t, MIX_W), lambda i, o=off // MIX_W: (i, 0, o))
    narrow = lambda off: pl.BlockSpec((1, t, LANES), lambda i, o=off // LANES: (i, 0, o))
    return pl.pallas_call(
        functools.partial(_dsa_prompt_kernel, t=t, topk=topk),
        grid=(b,),
        in_specs=[wide(OFF_QC), wide(OFF_QI), narrow(OFF_KC), narrow(OFF_VC), narrow(OFF_MISC - OFF_XD), narrow(OFF_MISC)],
        out_specs=pl.BlockSpec((1, t, MIX_W), lambda i: (i, 0, 0)),
        out_shape=jax.ShapeDtypeStruct((b, t, MIX_W), BF16),
        scratch_shapes=[pltpu.VMEM((tp, HEAD_DIM), BF16), pltpu.VMEM((tp, HEAD_DIM), BF16),
                        pltpu.VMEM((tp, LANES), BF16),
                        pltpu.VMEM((BAND_TILES * CHUNK, tp), I32), pltpu.VMEM((BAND_TILES * CHUNK, tp), F32)],
        compiler_params=_cparams("parallel"),
        name="dsa_prompt",
    )(proj16, proj16, proj16, proj16, tail32, proj16)


CONV_PAD = 8


def _rglru_kernel(xd_ref, cs_ref, h0_ref, cw_ref, cb_ref, wa_ref, ba_ref, wx_ref, bx_ref, lam_ref,
                  y_ref, hl_ref, xpad_scr, a_scr, u_scr, hs_scr, *, t):
    n_full, tail = t // CHUNK, t % CHUNK
    xpad_scr[pl.ds(0, CONV_PAD), :] = jnp.zeros((CONV_PAD, MIX_W), F32)
    xpad_scr[pl.ds(CONV_PAD - (CONV_W - 1), CONV_W - 1), :] = cs_ref[0]
    xpad_scr[pl.ds(CONV_PAD, t), :] = xd_ref[0]
    decay_rate = -RG_C * _softplus(-lam_ref[...])

    def chunk(r0, rows, h):
        win = xpad_scr[pl.ds(r0, rows + CONV_PAD), :]
        xc = cb_ref[...]
        for i in range(CONV_W):
            lo = CONV_PAD - (CONV_W - 1) + i
            xc = xc + win[lo:lo + rows] * cw_ref[i:i + 1, :]
        xc16 = xc.astype(BF16)
        gate_r = jax.nn.sigmoid(jnp.dot(xc16, wa_ref[...], preferred_element_type=F32) + ba_ref[...])
        gate_i = jax.nn.sigmoid(jnp.dot(xc16, wx_ref[...], preferred_element_type=F32) + bx_ref[...])
        log_a = gate_r * decay_rate
        a_scr[0:rows, :] = jnp.exp(log_a)
        th = jnp.tanh(log_a)
        u_scr[0:rows, :] = jnp.sqrt(-2.0 * th / (1.0 - th)) * gate_i * xc

        def step(r, hh):
            hh = a_scr[pl.ds(r, 1), :] * hh + u_scr[pl.ds(r, 1), :]
            hs_scr[pl.ds(r, 1), :] = hh
            return hh
        h = lax.fori_loop(0, rows, step, h, unroll=8)
        y_ref[0, pl.ds(r0, rows), :] = hs_scr[0:rows, :].astype(y_ref.dtype)
        return h

    h = h0_ref[0]
    if n_full:
        h = lax.fori_loop(0, n_full, lambda i, hh: chunk(pl.multiple_of(i * CHUNK, CHUNK), CHUNK, hh), h)
    if tail:
        h = chunk(n_full * CHUNK, tail, h)
    hl_ref[0] = h


def rglru(tail32, conv_state, h0, lw, out_dtype):
    b, t, _ = tail32.shape
    conv_w, conv_b, wa16, b_a, wx16, b_x, lam = (lw[k] for k in ("conv_w", "conv_b", "w_a", "b_a", "w_x", "b_x", "lam"))
    row = lambda v: v.reshape(1, MIX_W)
    const = lambda shape: pl.BlockSpec(shape, lambda i: (0,) * len(shape))
    return pl.pallas_call(
        functools.partial(_rglru_kernel, t=t),
        grid=(b,),
        in_specs=[pl.BlockSpec((1, t, MIX_W), lambda i: (i, 0, 0)),
                  pl.BlockSpec((1, CONV_W - 1, MIX_W), lambda i: (i, 0, 0)),
                  pl.BlockSpec((1, 1, MIX_W), lambda i: (i, 0, 0)),
                  const((CONV_W, MIX_W)), const((1, MIX_W)),
                  const((MIX_W, MIX_W)), const((1, MIX_W)),
                  const((MIX_W, MIX_W)), const((1, MIX_W)), const((1, MIX_W))],
        out_specs=[pl.BlockSpec((1, t, MIX_W), lambda i: (i, 0, 0)),
                   pl.BlockSpec((1, 1, MIX_W), lambda i: (i, 0, 0))],
        out_shape=[jax.ShapeDtypeStruct((b, t, MIX_W), out_dtype), jax.ShapeDtypeStruct((b, 1, MIX_W), F32)],
        scratch_shapes=[pltpu.VMEM((CONV_PAD + pl.cdiv(t, CHUNK) * CHUNK + CONV_PAD, MIX_W), F32),
                        pltpu.VMEM((CHUNK, MIX_W), F32), pltpu.VMEM((CHUNK, MIX_W), F32),
                        pltpu.VMEM((CHUNK, MIX_W), F32)],
        compiler_params=_cparams("parallel"),
        name="rglru",
    )(tail32, conv_state, h0.reshape(b, 1, MIX_W), conv_w, row(conv_b), wa16, row(b_a), wx16, row(b_x), row(lam))


def _block_diag(w):
    n, c, e = w.shape
    eye = jnp.eye(n, dtype=w.dtype)
    return (eye[:, None, :, None] * w[:, :, None, :]).reshape(n * c, n * e)


def _merge_kernel(h_ref, ya_ref, yb_ref, yc_ref, yd_ref, g0_ref, g1_ref, g2_ref, g3_ref, wb_ref, o_ref):
    h = h_ref[...]
    acc = None
    for n, (y_ref, g_ref) in enumerate(((ya_ref, g0_ref), (yb_ref, g1_ref), (yc_ref, g2_ref), (yd_ref, g3_ref))):
        gate = jax.nn.sigmoid(jnp.dot(h, g_ref[...], preferred_element_type=F32))
        term = gate * jnp.dot(y_ref[...], wb_ref[n], preferred_element_type=F32)
        acc = term if acc is None else acc + term
    o_ref[...] = acc.astype(o_ref.dtype)


def merge(h16, ys, wg16, wb16, tm_pref=512, tn=512):
    m, d = h16.shape
    tm = _row_tile(m, tm_pref)
    nj = d // tn
    y_spec = pl.BlockSpec((tm, MIX_W), lambda i, j: (i, 0))
    g_spec = lambda n: pl.BlockSpec((d, tn), lambda i, j, n=n: (0, n * nj + j))
    return pl.pallas_call(
        _merge_kernel,
        grid=(m // tm, nj),
        in_specs=[pl.BlockSpec((tm, d), lambda i, j: (i, 0)), y_spec, y_spec, y_spec, y_spec,
                  g_spec(0), g_spec(1), g_spec(2), g_spec(3),
                  pl.BlockSpec((4, MIX_W, tn), lambda i, j: (0, 0, j))],
        out_specs=pl.BlockSpec((tm, tn), lambda i, j: (i, j)),
        out_shape=jax.ShapeDtypeStruct((m, d), BF16),
        compiler_params=_cparams("parallel", "arbitrary"),
        name="merge",
    )(h16, *ys, wg16, wg16, wg16, wg16, wb16)


def _matmul_res_kernel(a_ref, w_ref, x_ref, o_ref):
    o_ref[...] = x_ref[...] + jnp.dot(a_ref[...], w_ref[...], preferred_element_type=F32)


def matmul_residual(a16, w16, x, tm_pref=344):
    m, k = a16.shape
    n = w16.shape[1]
    tm = _row_tile(m, tm_pref)
    return pl.pallas_call(
        _matmul_res_kernel,
        grid=(m // tm,),
        in_specs=[pl.BlockSpec((tm, k), lambda i: (i, 0)),
                  pl.BlockSpec((k, n), lambda i: (0, 0)),
                  pl.BlockSpec((tm, n), lambda i: (i, 0))],
        out_specs=pl.BlockSpec((tm, n), lambda i: (i, 0)),
        out_shape=jax.ShapeDtypeStruct((m, n), F32),
        compiler_params=_cparams("parallel"),
        name="out_proj",
    )(a16, w16, x)


def _ffn_kernel(x_ref, g_ref, wg_ref, wu_ref, wd_ref, o_ref, h_scr, acc_scr):
    f = pl.program_id(1)

    @pl.when(f == 0)
    def _():
        x = x_ref[...]
        y = x * lax.rsqrt(jnp.mean(x * x, axis=-1, keepdims=True) + RMS_EPS)
        h_scr[...] = (y * g_ref[...]).astype(BF16)
        acc_scr[...] = jnp.zeros_like(acc_scr)

    h = h_scr[...]
    gate = jnp.dot(h, wg_ref[...], preferred_element_type=F32)
    up = jnp.dot(h, wu_ref[...], preferred_element_type=F32)
    act = (jax.nn.silu(gate) * up).astype(BF16)
    acc_scr[...] += jnp.dot(act, wd_ref[...], preferred_element_type=F32)

    @pl.when(f == pl.num_programs(1) - 1)
    def _():
        o_ref[...] = x_ref[...] + acc_scr[...]


def ffn(x, g, wg16, wu16, wd16, tm_pref=688, tf=512):
    m, d = x.shape
    dff = wg16.shape[1]
    tm = _row_tile(m, tm_pref)
    return pl.pallas_call(
        _ffn_kernel,
        grid=(m // tm, dff // tf),
        in_specs=[pl.BlockSpec((tm, d), lambda i, f: (i, 0)),
                  pl.BlockSpec((1, d), lambda i, f: (0, 0)),
                  pl.BlockSpec((d, tf), lambda i, f: (0, f)),
                  pl.BlockSpec((d, tf), lambda i, f: (0, f)),
                  pl.BlockSpec((tf, d), lambda i, f: (f, 0))],
        out_specs=pl.BlockSpec((tm, d), lambda i, f: (i, 0)),
        out_shape=jax.ShapeDtypeStruct((m, d), F32),
        scratch_shapes=[pltpu.VMEM((tm, d), BF16), pltpu.VMEM((tm, d), F32)],
        compiler_params=_cparams("parallel", "arbitrary"),
        name="ffn",
    )(x, g.reshape(1, d), wg16, wu16, wd16)


def _rmsnorm_kernel(x_ref, g_ref, o_ref):
    x = x_ref[...]
    o_ref[...] = x * lax.rsqrt(jnp.mean(x * x, axis=-1, keepdims=True) + RMS_EPS) * g_ref[...]


def rmsnorm(x, g, tm_pref=512):
    m, d = x.shape
    tm = _row_tile(m, tm_pref)
    return pl.pallas_call(
        _rmsnorm_kernel,
        grid=(m // tm,),
        in_specs=[pl.BlockSpec((tm, d), lambda i: (i, 0)), pl.BlockSpec((1, d), lambda i: (0, 0))],
        out_specs=pl.BlockSpec((tm, d), lambda i: (i, 0)),
        out_shape=jax.ShapeDtypeStruct((m, d), F32),
        compiler_params=_cparams("parallel"),
        name="final_norm",
    )(x, g.reshape(1, d))


PAGES_PER_STEP = 32
STEP_KEYS = PAGES_PER_STEP * PAGE


def _page_specs(block, layer, n_pages, reverse):
    def spec(p):
        def index(b, s, pt):
            j = s * PAGES_PER_STEP + p
            if reverse:
                j = n_pages - 1 - j
            return (layer, pt[b, j], 0, 0)
        return pl.BlockSpec((1, 1) + block, index)
    return [spec(p) for p in range(PAGES_PER_STEP)]


def _per_seq(shape):
    return pl.BlockSpec((1,) + shape, lambda b, s, pt: (b,) + (0,) * len(shape))


def _head_page(ref, h):
    return ref.at[0, 0][pl.ds(h, PAGE, stride=N_HEADS), :].astype(BF16)


def _own_head_rows(per_head, t):
    return jnp.concatenate([per_head[h][h * t:(h + 1) * t] for h in range(N_HEADS)], axis=0)


def _query_index(t, width):
    return jnp.concatenate([_iota((t, width), 0)] * N_HEADS, axis=0)


def _pad_rows(dst, src):
    dst[...] = jnp.zeros(dst.shape, dst.dtype)
    dst[pl.ds(0, src.shape[0]), :] = src


def _per_head_rows(x, t):
    return jnp.concatenate([jnp.broadcast_to(x[h:h + 1], (t, x.shape[1])) for h in range(N_HEADS)], axis=0)


def _per_head_cols(x):
    return jnp.concatenate([x[:, h:h + 1] for h in range(N_HEADS)], axis=0)


def _write_stacked(o_ref, out, t):
    for h in range(N_HEADS):
        o_ref[0, :, h * HEAD_DIM:(h + 1) * HEAD_DIM] = out[h * t:(h + 1) * t].astype(o_ref.dtype)


def _softmax_update(carry, s, pv_fn):
    m, l, acc = carry
    m_new = jnp.maximum(m, jnp.max(s, axis=1, keepdims=True))
    m_safe = jnp.where(m_new == NEG_INF, 0.0, m_new)
    alpha = jnp.exp(m - m_safe)
    p = jnp.exp(s - m_safe)
    l = alpha * l + jnp.sum(p, axis=1, keepdims=True)
    return m_new, l, alpha * acc + pv_fn(p.astype(BF16))


def _softmax_init(rows):
    return (jnp.full((rows, 1), NEG_INF, F32), jnp.zeros((rows, 1), F32), jnp.zeros((rows, HEAD_DIM), F32))


def _new_keys_per_head(q4, pad_ref, t):
    return _own_head_rows([_dot_nt(q4, pad_ref[:, h * HEAD_DIM:(h + 1) * HEAD_DIM].astype(BF16))
                           for h in range(N_HEADS)], t)


def _new_values_per_head(p16, pad_ref, t):
    return _own_head_rows([jnp.dot(p16, pad_ref[:, h * HEAD_DIM:(h + 1) * HEAD_DIM].astype(BF16),
                                   preferred_element_type=F32) for h in range(N_HEADS)], t)


def _page_scores(q4, k_refs, t):
    return _own_head_rows([_dot_nt(q4, jnp.concatenate([_head_page(k_ref, h) for k_ref in k_refs], axis=0))
                           for h in range(N_HEADS)], t)


def _page_values(p16, v_refs, t):
    return _own_head_rows([jnp.dot(p16, jnp.concatenate([_head_page(v_ref, h) for v_ref in v_refs], axis=0),
                                   preferred_element_type=F32) for h in range(N_HEADS)], t)


def _fox_sample_kernel(pt_ref, q_ref, kn_ref, vn_ref, cn_ref, cnrow_ref, *rest, t):
    n = PAGES_PER_STEP
    k_refs, v_refs, lf_refs = rest[0:n], rest[n:2 * n], rest[2 * n:3 * n]
    o_ref, kpad, vpad, m_scr, l_scr, acc_scr, d_scr = rest[3 * n:]
    s_id = pl.program_id(1)
    scale = HEAD_DIM ** -0.5
    rows = N_HEADS * t
    q4 = _stack_heads(q_ref[0]).astype(BF16)
    cn = _per_head_cols(cn_ref[0])

    @pl.when(s_id == 0)
    def _():
        _pad_rows(kpad, kn_ref[0])
        _pad_rows(vpad, vn_ref[0])
        s = _new_keys_per_head(q4, kpad, t) * scale + cn - _per_head_rows(cnrow_ref[0], t)
        col = _iota((rows, CHUNK), 1)
        s = jnp.where((col <= _query_index(t, CHUNK)) & (col < t), s, NEG_INF)
        m, l, acc = _softmax_update(_softmax_init(rows), s, lambda p16: _new_values_per_head(p16, vpad, t))
        m_scr[...], l_scr[...], acc_scr[...] = m, l, acc
        d_scr[...] = jnp.zeros(d_scr.shape, F32)

    suffix = _suffix_matrix()
    later = d_scr[...]
    decays = []
    for p in range(n):
        lf = jnp.concatenate([lf_refs[p][0, 0], jnp.zeros((SUBLANES - N_HEADS, PAGE), F32)], axis=0)
        decays.append(_dot01_right(lf, suffix) + later)
        later = later + jnp.sum(lf, axis=1, keepdims=True)
    decay = jnp.concatenate(decays, axis=1)
    s = _page_scores(q4, k_refs, t) * scale + _per_head_rows(decay[0:N_HEADS], t) + cn
    carry = _softmax_update((m_scr[...], l_scr[...], acc_scr[...]), s, lambda p16: _page_values(p16, v_refs, t))
    m_scr[...], l_scr[...], acc_scr[...] = carry
    d_scr[...] = later

    @pl.when(s_id == pl.num_programs(1) - 1)
    def _():
        _write_stacked(o_ref, carry[2] / carry[1], t)


def _kv_page_block():
    return (PAGE * N_HEADS, HEAD_DIM)


def fox_sample(page_table, layer, q, kn, vn, cn, cnrow, cache_k, cache_v, cache_lft):
    b, t, _ = q.shape
    n_pages = page_table.shape[1]
    rows = N_HEADS * t
    grid_spec = pltpu.PrefetchScalarGridSpec(
        num_scalar_prefetch=1,
        grid=(b, n_pages // PAGES_PER_STEP),
        in_specs=[_per_seq((t, MIX_W)), _per_seq((t, MIX_W)), _per_seq((t, MIX_W)),
                  _per_seq((t, N_HEADS)), _per_seq((N_HEADS, CHUNK))]
                 + _page_specs(_kv_page_block(), layer, n_pages, True)
                 + _page_specs(_kv_page_block(), layer, n_pages, True)
                 + _page_specs((N_HEADS, PAGE), layer, n_pages, True),
        out_specs=_per_seq((t, MIX_W)),
        scratch_shapes=[pltpu.VMEM((CHUNK, MIX_W), F32), pltpu.VMEM((CHUNK, MIX_W), F32),
                        pltpu.VMEM((rows, 1), F32), pltpu.VMEM((rows, 1), F32), pltpu.VMEM((rows, HEAD_DIM), F32),
                        pltpu.VMEM((SUBLANES, 1), F32)])
    return pl.pallas_call(
        functools.partial(_fox_sample_kernel, t=t),
        grid_spec=grid_spec,
        out_shape=jax.ShapeDtypeStruct((b, t, MIX_W), F32),
        compiler_params=_cparams("parallel", "arbitrary"),
        name="fox_sample",
    )(page_table, q, kn, vn, cn, cnrow, *([cache_k] * PAGES_PER_STEP), *([cache_v] * PAGES_PER_STEP),
      *([cache_lft] * PAGES_PER_STEP))


def _sb_sample_kernel(pt_ref, q_ref, kn_ref, vn_ref, *rest, t):
    n = PAGES_PER_STEP
    k_refs, v_refs = rest[0:n], rest[n:2 * n]
    o_ref, kpad, vpad, r_scr, acc_scr = rest[2 * n:]
    s_id = pl.program_id(1)
    scale = HEAD_DIM ** -0.5
    rows = N_HEADS * t
    q4 = _stack_heads(q_ref[0]).astype(BF16)
    suffix = _suffix_matrix()

    @pl.when(s_id == 0)
    def _():
        _pad_rows(kpad, kn_ref[0])
        _pad_rows(vpad, vn_ref[0])
        z = _new_keys_per_head(q4, kpad, t) * scale
        vis = _iota((rows, CHUNK), 1) < _query_index(t, CHUNK)
        log_keep = jnp.where(vis, -_softplus_abs(z), 0.0)
        att = jnp.where(vis, jnp.exp(log_keep + z + _dot01_right(log_keep, suffix, SB_SPLIT)), 0.0)
        acc_scr[...] = _new_values_per_head(att.astype(BF16), vpad, t)
        r_scr[...] = jnp.sum(log_keep, axis=1, keepdims=True)

    later = r_scr[...]
    z = _page_scores(q4, k_refs, t) * scale
    log_keep = -_softplus_abs(z)
    page = lambda x, p: x[:, p * PAGE:(p + 1) * PAGE]
    in_page = _dot01_right(jnp.concatenate([page(log_keep, p) for p in range(n)], axis=0), suffix, SB_SPLIT)
    pieces = []
    for p in range(n):
        suf = in_page[p * rows:(p + 1) * rows]
        pieces.append(jnp.exp(page(log_keep, p) + page(z, p) + suf + later).astype(BF16))
        later = later + suf[:, 0:1] + page(log_keep, p)[:, 0:1]
    acc = acc_scr[...] + _page_values(jnp.concatenate(pieces, axis=1), v_refs, t)
    r_scr[...], acc_scr[...] = later, acc

    @pl.when(s_id == pl.num_programs(1) - 1)
    def _():
        _write_stacked(o_ref, acc, t)


def sb_sample(page_table, layer, q, kn, vn, cache_k, cache_v):
    b, t, _ = q.shape
    n_pages = page_table.shape[1]
    rows = N_HEADS * t
    grid_spec = pltpu.PrefetchScalarGridSpec(
        num_scalar_prefetch=1,
        grid=(b, n_pages // PAGES_PER_STEP),
        in_specs=[_per_seq((t, MIX_W))] * 3
                 + _page_specs(_kv_page_block(), layer, n_pages, True)
                 + _page_specs(_kv_page_block(), layer, n_pages, True),
        out_specs=_per_seq((t, MIX_W)),
        scratch_shapes=[pltpu.VMEM((CHUNK, MIX_W), F32), pltpu.VMEM((CHUNK, MIX_W), F32),
                        pltpu.VMEM((rows, 1), F32), pltpu.VMEM((rows, HEAD_DIM), F32)])
    return pl.pallas_call(
        functools.partial(_sb_sample_kernel, t=t),
        grid_spec=grid_spec,
        out_shape=jax.ShapeDtypeStruct((b, t, MIX_W), F32),
        compiler_params=_cparams("parallel", "arbitrary"),
        name="sb_sample",
    )(page_table, q, kn, vn, *([cache_k] * PAGES_PER_STEP), *([cache_v] * PAGES_PER_STEP))


def _weighted_relu_sum(g, wi, t):
    score = None
    for h in range(IDX_HEADS):
        term = wi[:, h:h + 1] * jnp.maximum(g[h * t:(h + 1) * t], 0.0)
        score = term if score is None else score + term
    return score * ((IDX_DIM * IDX_HEADS) ** -0.5)


def _dsa_scores_kernel(pt_ref, qi_ref, misc_ref, *rest, t):
    n = PAGES_PER_STEP
    idx_refs = rest[0:n]
    past_ref, new_ref, kpad = rest[n:]
    qi = qi_ref[0]
    qi_stack = jnp.concatenate([qi[:, h * IDX_DIM:(h + 1) * IDX_DIM] for h in range(IDX_HEADS)], axis=0).astype(BF16)
    wi = misc_ref[0, :, MISC_WI:MISC_WI + IDX_HEADS]

    @pl.when(pl.program_id(1) == 0)
    def _():
        _pad_rows(kpad, misc_ref[0])
        g = _dot_nt(qi_stack, kpad[:, MISC_KI:MISC_KI + IDX_DIM].astype(BF16))
        new_ref[0] = _weighted_relu_sum(g, wi, t)

    for p in range(n):
        g = jnp.dot(qi_stack, idx_refs[p][0, 0].astype(BF16), preferred_element_type=F32)
        past_ref[0, :, p * PAGE:(p + 1) * PAGE] = _weighted_relu_sum(g, wi, t)


def dsa_sample_scores(page_table, layer, qi, misc, cache_idx_t):
    b, t, _ = qi.shape
    n_pages = page_table.shape[1]
    grid_spec = pltpu.PrefetchScalarGridSpec(
        num_scalar_prefetch=1,
        grid=(b, n_pages // PAGES_PER_STEP),
        in_specs=[_per_seq((t, MIX_W)), _per_seq((t, LANES))] + _page_specs((IDX_DIM, PAGE), layer, n_pages, False),
        out_specs=[pl.BlockSpec((1, t, STEP_KEYS), lambda i, s, pt: (i, 0, s)), _per_seq((t, CHUNK))],
        scratch_shapes=[pltpu.VMEM((CHUNK, LANES), F32)])
    return pl.pallas_call(
        functools.partial(_dsa_scores_kernel, t=t),
        grid_spec=grid_spec,
        out_shape=[jax.ShapeDtypeStruct((b, t, n_pages * PAGE), F32), jax.ShapeDtypeStruct((b, t, CHUNK), F32)],
        compiler_params=_cparams("parallel", "arbitrary"),
        name="dsa_sample_scores",
    )(page_table, qi, misc, *([cache_idx_t] * PAGES_PER_STEP))


SELECT_ROWS = 64


def _dsa_select_kernel(score_ref, bias_ref, key_scr, *, t, n_past, topk):
    rows, width = score_ref.shape
    qidx = jnp.concatenate([_iota((t, 1), 0)] * (rows // t), axis=0)

    def vis_fn(c0, w):
        col = c0 + _iota((rows, w), 1)
        return (col < n_past) | ((col - n_past <= qidx) & (col - n_past < t))

    key_scr[...] = jnp.where(vis_fn(0, width), _order_key(score_ref[...]), INT_MIN)
    _select_topk(key_scr, bias_ref, vis_fn, rows, width, topk)


def dsa_sample_select(scores, t, n_past, topk):
    m, width = scores.shape
    rows = _row_tile(m, SELECT_ROWS)
    assert rows % t == 0
    spec = pl.BlockSpec((rows, width), lambda i: (i, 0))
    return pl.pallas_call(
        functools.partial(_dsa_select_kernel, t=t, n_past=n_past, topk=topk),
        grid=(m // rows,),
        in_specs=[spec],
        out_specs=spec,
        out_shape=jax.ShapeDtypeStruct(scores.shape, F32),
        scratch_shapes=[pltpu.VMEM((rows, width), I32)],
        compiler_params=_cparams("parallel"),
        name="dsa_sample_select",
    )(scores)


def _dsa_attend_kernel(pt_ref, q_ref, kn_ref, vn_ref, bias_new_ref, bias_past_ref, *rest, t):
    n = PAGES_PER_STEP
    k_refs, v_refs = rest[0:n], rest[n:2 * n]
    o_ref, kpad, vpad, m_scr, l_scr, acc_scr = rest[2 * n:]
    s_id = pl.program_id(1)
    scale = HEAD_DIM ** -0.5
    rows = N_HEADS * t
    q4 = _stack_heads(q_ref[0]).astype(BF16)

    @pl.when(s_id == 0)
    def _():
        _pad_rows(kpad, kn_ref[0])
        _pad_rows(vpad, vn_ref[0])
        s = _dot_nt(q4, kpad[...].astype(BF16)) * scale + jnp.concatenate([bias_new_ref[0]] * N_HEADS, axis=0)
        m, l, acc = _softmax_update(
            _softmax_init(rows), s,
            lambda p16: jnp.dot(p16, vpad[...].astype(BF16), preferred_element_type=F32))
        m_scr[...], l_scr[...], acc_scr[...] = m, l, acc

    step_rows = lambda refs: jnp.concatenate([r[0, 0].astype(BF16) for r in refs], axis=0)
    s = _dot_nt(q4, step_rows(k_refs)) * scale + jnp.concatenate([bias_past_ref[0]] * N_HEADS, axis=0)
    carry = _softmax_update((m_scr[...], l_scr[...], acc_scr[...]), s,
                            lambda p16: jnp.dot(p16, step_rows(v_refs), preferred_element_type=F32))
    m_scr[...], l_scr[...], acc_scr[...] = carry

    @pl.when(s_id == pl.num_programs(1) - 1)
    def _():
        _write_stacked(o_ref, carry[2] / carry[1], t)


def dsa_sample_attend(page_table, layer, q, kn, vn, bias, cache_k, cache_v):
    b, t, _ = q.shape
    n_pages = page_table.shape[1]
    rows = N_HEADS * t
    grid_spec = pltpu.PrefetchScalarGridSpec(
        num_scalar_prefetch=1,
        grid=(b, n_pages // PAGES_PER_STEP),
        in_specs=[_per_seq((t, MIX_W)), _per_seq((t, HEAD_DIM)), _per_seq((t, HEAD_DIM)),
                  pl.BlockSpec((1, t, CHUNK), lambda i, s, pt: (i, 0, n_pages)),
                  pl.BlockSpec((1, t, STEP_KEYS), lambda i, s, pt: (i, 0, s))]
                 + _page_specs((PAGE, HEAD_DIM), layer, n_pages, False)
                 + _page_specs((PAGE, HEAD_DIM), layer, n_pages, False),
        out_specs=_per_seq((t, MIX_W)),
        scratch_shapes=[pltpu.VMEM((CHUNK, HEAD_DIM), F32), pltpu.VMEM((CHUNK, HEAD_DIM), F32),
                        pltpu.VMEM((rows, 1), F32), pltpu.VMEM((rows, 1), F32), pltpu.VMEM((rows, HEAD_DIM), F32)])
    return pl.pallas_call(
        functools.partial(_dsa_attend_kernel, t=t),
        grid_spec=grid_spec,
        out_shape=jax.ShapeDtypeStruct((b, t, MIX_W), F32),
        compiler_params=_cparams("parallel", "arbitrary"),
        name="dsa_sample_attend",
    )(page_table, q, kn, vn, bias, bias, *([cache_k] * PAGES_PER_STEP), *([cache_v] * PAGES_PER_STEP))


def _pad_w_in(w):
    widths = (512, 512, 512, 4, 512, 512, 512, 512, 128, 128, 512, 64, 8, 512)
    names = ("qa", "ka", "va", "fa", "qb", "kb", "vb", "qc", "kc", "vc", "qi", "ki", "wi", "xd")
    seg, off = {}, 0
    for name, width in zip(names, widths):
        seg[name] = w[:, :, off:off + width]
        off += width
    order = ("qa", "ka", "va", "qb", "kb", "vb", "qc", "qi", "xd", "kc", "vc", "ki", "fa", "wi")
    cols = jnp.concatenate([seg[n] for n in order], axis=2)
    return jnp.pad(cols, ((0, 0), (0, 0), (0, N_PROJ - cols.shape[2]))).astype(BF16)


def _cols(p, off, width):
    return p[..., off:off + width]


def _tail_cols(tail32, off, width):
    return _cols(tail32, off - OFF_XD, width)


def _small_rows(tail32, lf):
    return (_cols(lf, MISC_FA, N_HEADS), _tail_cols(tail32, OFF_KC, HEAD_DIM),
            _tail_cols(tail32, OFF_VC, HEAD_DIM), _tail_cols(tail32, OFF_MISC + MISC_KI, IDX_DIM))


def _assemble_rows(head_rows, small_rows, b, t):
    ka, va, kb, vb = (r.reshape(r.shape[0], b, t, N_HEADS, HEAD_DIM) for r in head_rows)
    logf, kc, vc, ki, conv, h_last = (jnp.stack(list(r)) for r in zip(*small_rows))
    return (ka, va, logf, kb, vb, kc, vc, ki, conv, h_last)


def _dense_tail(x2d, h16, ys, lw):
    merged = merge(h16, ys, lw["w_gate"], lw["w_branch"])
    x2d = matmul_residual(merged, lw["w_out"], x2d)
    return ffn(x2d, lw["g_ffn"], lw["w_fg"], lw["w_fu"], lw["w_fd"])


def _forget_bias_row(b_f):
    return jnp.zeros((1, LANES), F32).at[0, MISC_FA:MISC_FA + N_HEADS].set(b_f)


def _conv_tail(conv_state, tail32):
    xd = _tail_cols(tail32, OFF_XD, MIX_W)
    return jnp.concatenate([conv_state, xd], axis=1)[:, -(CONV_W - 1):]


def _prompt_layer(x2d, b, t, lw, topk, layer, depth, prev_head_rows):
    tail32, p16, h16, *head_rows = norm_proj(x2d, lw["g_mix"], lw["w_in_t"], layer, depth, prev_head_rows)
    tail32, p16 = tail32.reshape(b, t, N_TAIL), p16.reshape(b, t, N_PROJ)
    lf, fc = logf_cumsum(_tail_cols(tail32, OFF_MISC, LANES), _forget_bias_row(lw["b_f"]))
    tp = pl.cdiv(t, CHUNK) * CHUNK
    frow = jnp.pad(_cols(fc, MISC_FA, N_HEADS), ((0, 0), (0, tp - t), (0, 0))).transpose(0, 2, 1)
    conv0 = jnp.zeros((b, CONV_W - 1, MIX_W), F32)
    ya = fox_prompt(p16, frow.reshape(b, N_HEADS, 1, tp))
    yb = sb_prompt(p16)
    yc = dsa_prompt(tail32, p16, topk)
    yd, h_last = rglru(tail32, conv0, jnp.zeros((b, MIX_W), F32), lw, BF16)
    ys = [y.reshape(b * t, MIX_W) for y in (ya, yb, yc, yd)]
    rows = _small_rows(tail32, lf) + (_conv_tail(conv0, tail32), h_last.reshape(b, MIX_W))
    return _dense_tail(x2d, h16, ys, lw), rows, head_rows


def _sample_layer(x2d, b, t, lw, topk, layer, depth, prev_head_rows, page_table, caches, conv_state, h0):
    ca_k, ca_v, ca_lft, cb_k, cb_v, cc_k, cc_v, cc_idx_t = caches
    n_past = page_table.shape[1] * PAGE
    tail32, p16, h16, *head_rows = norm_proj(x2d, lw["g_mix"], lw["w_in_t"], layer, depth, prev_head_rows)
    tail32 = tail32.reshape(b, t, N_TAIL)
    seg = lambda off, width: _cols(p16, off, width).astype(F32).reshape(b, t, width)
    misc = _tail_cols(tail32, OFF_MISC, LANES)
    lf, fc = logf_cumsum(misc, _forget_bias_row(lw["b_f"]))
    cn = _cols(fc, MISC_FA, N_HEADS)
    cnrow = jnp.pad(cn.transpose(0, 2, 1), ((0, 0), (0, 0), (0, CHUNK - t)))
    ya = fox_sample(page_table, layer, seg(OFF_QA, MIX_W), seg(OFF_KA, MIX_W), seg(OFF_VA, MIX_W),
                    cn, cnrow, ca_k, ca_v, ca_lft)
    yb = sb_sample(page_table, layer, seg(OFF_QB, MIX_W), seg(OFF_KB, MIX_W), seg(OFF_VB, MIX_W), cb_k, cb_v)
    score_past, score_new = dsa_sample_scores(page_table, layer, seg(OFF_QI, MIX_W), misc, cc_idx_t)
    scores = jnp.concatenate([score_past, score_new], axis=-1).reshape(b * t, n_past + CHUNK)
    bias = dsa_sample_select(scores, t, n_past, topk).reshape(b, t, n_past + CHUNK)
    yc = dsa_sample_attend(page_table, layer, seg(OFF_QC, MIX_W), seg(OFF_KC, HEAD_DIM), seg(OFF_VC, HEAD_DIM),
                           bias, cc_k, cc_v)
    yd, h_last = rglru(tail32, conv_state, h0, lw, F32)
    ys = [y.reshape(b * t, MIX_W).astype(BF16) for y in (ya, yb, yc, yd)]
    rows = _small_rows(tail32, lf) + (_conv_tail(conv_state, tail32), h_last.reshape(b, MIX_W))
    return _dense_tail(x2d, h16, ys, lw), rows, head_rows


def kernel(x_prompt, x_sample, cache_a_k, cache_a_v, cache_a_logf, cache_b_k, cache_b_v, cache_c_k, cache_c_v, cache_c_idx_k, state_d_conv, state_d_h, page_table, meta_tokens, w_in, b_forget, conv_w, conv_b, w_rg_a, b_rg_a, w_rg_x, b_rg_x, rg_lambda, w_gate, w_branch, w_out, norm_mix, norm_ffn, w_ffn_gate, w_ffn_up, w_ffn_down, norm_final):
    depth = w_in.shape[0]
    n_p, seq_p, _ = x_prompt.shape
    n_dec, t_dec, _ = x_sample.shape
    n_pool = cache_a_k.shape[1]
    past_len = page_table.shape[1] * PAGE
    topk_p = min(TOPK_MAX, seq_p // 4)
    topk_s = min(TOPK_MAX, (past_len + t_dec) // 4)

    rows_view = lambda c: c.reshape(depth, n_pool, PAGE * N_HEADS, HEAD_DIM)
    caches = (rows_view(cache_a_k), rows_view(cache_a_v), cache_a_logf.transpose(0, 1, 3, 2),
              rows_view(cache_b_k), rows_view(cache_b_v), cache_c_k, cache_c_v, cache_c_idx_k.transpose(0, 1, 3, 2))

    xp = jnp.concatenate([jnp.broadcast_to(meta_tokens[None], (n_p, N_META, D_MODEL)), x_prompt], axis=1)
    t_p = seq_p + N_META
    xp = xp.reshape(n_p * t_p, D_MODEL)
    xs = x_sample.reshape(n_dec * t_dec, D_MODEL)

    w_in_p = _pad_w_in(w_in)
    prompt_rows, sample_rows, prompt_heads, sample_heads = [], [], None, None
    for l in range(depth):
        lw = dict(w_in_t=w_in_p[l].T, b_f=b_forget[l], conv_w=conv_w[l], conv_b=conv_b[l],
                  w_a=_block_diag(w_rg_a[l]).astype(BF16), b_a=b_rg_a[l],
                  w_x=_block_diag(w_rg_x[l]).astype(BF16), b_x=b_rg_x[l], lam=rg_lambda[l],
                  w_gate=w_gate[l].astype(BF16), w_branch=w_branch[l].astype(BF16), w_out=w_out[l].astype(BF16),
                  g_mix=norm_mix[l], g_ffn=norm_ffn[l], w_fg=w_ffn_gate[l].astype(BF16),
                  w_fu=w_ffn_up[l].astype(BF16), w_fd=w_ffn_down[l].astype(BF16))
        xp, rows, prompt_heads = _prompt_layer(xp, n_p, t_p, lw, topk_p, l, depth, prompt_heads)
        prompt_rows.append(rows)
        xs, rows, sample_heads = _sample_layer(xs, n_dec, t_dec, lw, topk_s, l, depth, sample_heads, page_table, caches,
                                               state_d_conv[l], state_d_h[l])
        sample_rows.append(rows)

    y_prompt = rmsnorm(xp, norm_final).reshape(n_p, t_p, D_MODEL)[:, N_META:]
    y_sample = rmsnorm(xs, norm_final).reshape(n_dec, t_dec, D_MODEL)
    return ((y_prompt, y_sample) + _assemble_rows(prompt_heads, prompt_rows, n_p, t_p)
            + _assemble_rows(sample_heads, sample_rows, n_dec, t_dec))
```
